```python
import jax, jax.numpy as jnp
from jax import lax
import numpy as np

D_MODEL = 1024
BATCH = 8
SEQ = 4096
DEPTH = 4

D_MIX = D_MODEL
SB_HEAD_DIM = 64
SB_HEADS = (D_MIX // 2) // SB_HEAD_DIM
SB_WIDTH = SB_HEADS * SB_HEAD_DIM
SB_BLOCK = 128
GLA_HEADS = 4
GLA_DV = (D_MIX // 2) // GLA_HEADS
GLA_DK = GLA_DV // 2
GLA_K_WIDTH = GLA_HEADS * GLA_DK
GLA_V_WIDTH = GLA_HEADS * GLA_DV
GLA_RANK = 16
GLA_GATE_NORM = 16.0
GLA_CHUNK = 64
D_IN_PROJ = 3 * SB_WIDTH + 2 * GLA_K_WIDTH + 2 * GLA_V_WIDTH + GLA_RANK
N_GROUPS = 4
EXPERTS_PER_GROUP = 8
TOP_K_IN_GROUP = 2
D_EXPERT = 256
N_MOD = 6
EPS = 1e-6

kernel_name = "hybrid_sb_gla_hmoe_adaln"


def rms_norm(x, gain):
    xf = x.astype(jnp.float32)
    y = xf * lax.rsqrt(jnp.mean(xf * xf, axis=-1, keepdims=True) + EPS)
    return (y * gain.astype(jnp.float32)).astype(x.dtype)


def split_heads(t, n_heads):
    b, s, w = t.shape
    return t.reshape(b, s, n_heads, w // n_heads).transpose(0, 2, 1, 3)


def merge_heads(t):
    b, h, s, d = t.shape
    return t.transpose(0, 2, 1, 3).reshape(b, s, h * d)


def stick_breaking_attention(q, k, v):
    B, H, S, d = q.shape
    nb = S // SB_BLOCK
    scale = d ** -0.5
    q_blocks = jnp.moveaxis(q.reshape(B, H, nb, SB_BLOCK, d), 2, 0)
    key_pos = jnp.arange(S)

    def one_block(args):
        q_blk, blk = args
        z = jnp.einsum('bhqd,bhkd->bhqk', q_blk, k).astype(jnp.float32) * scale
        query_pos = blk * SB_BLOCK + jnp.arange(SB_BLOCK)
        before = key_pos[None, :] < query_pos[:, None]
        log_fail = jnp.where(before, -jax.nn.softplus(z), 0.0)
        suffix = lax.cumsum(log_fail, axis=3, reverse=True) - log_fail
        w = jnp.where(before, jnp.exp(jax.nn.log_sigmoid(z) + suffix), 0.0)
        return jnp.einsum('bhqk,bhkd->bhqd', w.astype(v.dtype), v)

    out = lax.map(one_block, (q_blocks, jnp.arange(nb)))
    return jnp.moveaxis(out, 0, 2).reshape(B, H, S, d)


def gla_chunked(q, k, v, log_g):
    B, H, S, dk = q.shape
    dv = v.shape[-1]
    C = GLA_CHUNK
    nc = S // C
    f32 = jnp.float32
    qc = q.astype(f32).reshape(B, H, nc, C, dk) * (dk ** -0.5)
    kc = k.astype(f32).reshape(B, H, nc, C, dk)
    vc = v.astype(f32).reshape(B, H, nc, C, dv)
    b = jnp.cumsum(log_g.astype(f32).reshape(B, H, nc, C, dk), axis=3)
    b_last = b[:, :, :, -1:, :]
    q_dec = qc * jnp.exp(b)
    k_inv = kc * jnp.exp(-b)
    causal = jnp.tril(jnp.ones((C, C), dtype=bool))
    scores = jnp.where(causal, jnp.einsum('bhnqd,bhnkd->bhnqk', q_dec, k_inv), 0.0)
    intra = jnp.einsum('bhnqk,bhnkv->bhnqv', scores, vc)
    chunk_kv = jnp.einsum('bhnkd,bhnkv->bhndv', kc * jnp.exp(b_last - b), vc)
    chunk_decay = jnp.exp(b_last[:, :, :, 0, :])

    def step(state, inp):
        dec, kv_c = inp
        return dec[..., None] * state + kv_c, state

    _, prev = lax.scan(step, jnp.zeros((B, H, dk, dv), f32),
                       (jnp.moveaxis(chunk_decay, 2, 0), jnp.moveaxis(chunk_kv, 2, 0)))
    prev = jnp.moveaxis(prev, 0, 2)
    inter = jnp.einsum('bhnqd,bhndv->bhnqv', q_dec, prev)
    return (intra + inter).reshape(B, H, S, dv).astype(v.dtype)


def hybrid_mixer(h, w_in, w_gk2, b_gk, q_gain, k_gain, gla_gain, w_out):
    proj = h @ w_in
    widths = (SB_WIDTH, SB_WIDTH, SB_WIDTH, GLA_K_WIDTH, GLA_K_WIDTH,
              GLA_V_WIDTH, GLA_V_WIDTH, GLA_RANK)
    cuts = np.cumsum(widths)[:-1].tolist()
    sb_q, sb_k, sb_v, g_q, g_k, g_v, g_r, g_lr = jnp.split(proj, cuts, axis=-1)
    qa = rms_norm(split_heads(sb_q, SB_HEADS), q_gain)
    ka = rms_norm(split_heads(sb_k, SB_HEADS), k_gain)
    out_a = merge_heads(stick_breaking_attention(qa, ka, split_heads(sb_v, SB_HEADS)))
    log_g = jax.nn.log_sigmoid((g_lr @ w_gk2 + b_gk).astype(jnp.float32)) / GLA_GATE_NORM
    o_b = gla_chunked(split_heads(g_q, GLA_HEADS), split_heads(g_k, GLA_HEADS),
                      split_heads(g_v, GLA_HEADS), split_heads(log_g, GLA_HEADS))
    out_b = merge_heads(rms_norm(o_b, gla_gain)) * jax.nn.silu(g_r)
    return jnp.concatenate([out_a, out_b], axis=-1) @ w_out


def hierarchical_moe(h, w_router_grp, b_router_grp, w_router_exp, b_router_exp,
                     w_gate, w_up, w_down):
    B, S, D = h.shape
    hf = h.reshape(B * S, D)
    f32 = jnp.float32
    grp_prob = jax.nn.softmax((hf @ w_router_grp + b_router_grp).astype(f32), axis=-1)
    grp_w, grp_idx = lax.top_k(grp_prob, 1)
    exp_logits = (hf @ w_router_exp + b_router_exp).astype(f32).reshape(-1, N_GROUPS, EXPERTS_PER_GROUP)
    sel_logits = jnp.take_along_axis(exp_logits, grp_idx[:, :, None], axis=1)[:, 0]
    top_logit, top_idx = lax.top_k(sel_logits, TOP_K_IN_GROUP)
    top_w = jax.nn.softmax(top_logit, axis=-1) * grp_w
    exp_w = jnp.sum(jax.nn.one_hot(top_idx, EXPERTS_PER_GROUP, dtype=f32) * top_w[..., None], axis=1)
    comb = (jax.nn.one_hot(grp_idx[:, 0], N_GROUPS, dtype=f32)[:, :, None]
            * exp_w[:, None, :]).astype(h.dtype)
    y = jnp.zeros_like(hf)
    for g in range(N_GROUPS):
        a = jnp.einsum('nd,edf->nef', hf, w_gate[g])
        u = jnp.einsum('nd,edf->nef', hf, w_up[g])
        y = y + jnp.einsum('nef,efd->nd', jax.nn.silu(a) * u * comb[:, g, :, None], w_down[g])
    return y.reshape(B, S, D)


def setup_inputs(seed: int = 0) -> dict:
    key = jax.random.key(seed)
    ks = jax.random.split(key, 20)
    f32 = jnp.float32
    nrm = lambda k, shape, s: jax.random.normal(k, shape, f32) * s
    L, D = DEPTH, D_MODEL
    return {
        "x": nrm(ks[0], (BATCH, SEQ, D), 1.0),
        "c": nrm(ks[1], (BATCH, D), 1.0),
        "w_ada": nrm(ks[2], (L, D, N_MOD * D), 0.5 * D ** -0.5),
        "b_ada": nrm(ks[3], (L, N_MOD * D), 0.02),
        "norm_mix": 1.0 + nrm(ks[4], (L, D), 0.02),
        "norm_ffn": 1.0 + nrm(ks[5], (L, D), 0.02),
        "w_in": nrm(ks[6], (L, D, D_IN_PROJ), D ** -0.5),
        "w_gk2": nrm(ks[7], (L, GLA_RANK, GLA_K_WIDTH), GLA_RANK ** -0.5),
        "b_gk": 2.0 + nrm(ks[8], (L, GLA_K_WIDTH), 0.1),
        "q_gain": 1.0 + nrm(ks[9], (L, SB_HEAD_DIM), 0.02),
        "k_gain": 1.0 + nrm(ks[10], (L, SB_HEAD_DIM), 0.02),
        "gla_gain": 1.0 + nrm(ks[11], (L, GLA_DV), 0.02),
        "w_out": nrm(ks[12], (L, D_MIX, D), D_MIX ** -0.5),
        "w_router_grp": nrm(ks[13], (L, D, N_GROUPS), D ** -0.5),
        "b_router_grp": nrm(ks[14], (L, N_GROUPS), 0.01),
        "w_router_exp": nrm(ks[15], (L, D, N_GROUPS * EXPERTS_PER_GROUP), D ** -0.5),
        "b_router_exp": nrm(ks[16], (L, N_GROUPS * EXPERTS_PER_GROUP), 0.01),
        "w_gate": nrm(ks[17], (L, N_GROUPS, EXPERTS_PER_GROUP, D, D_EXPERT), D ** -0.5),
        "w_up": nrm(ks[18], (L, N_GROUPS, EXPERTS_PER_GROUP, D, D_EXPERT), D ** -0.5),
        "w_down": nrm(ks[19], (L, N_GROUPS, EXPERTS_PER_GROUP, D_EXPERT, D), D_EXPERT ** -0.5),
    }


def reference(x, c, w_ada, b_ada, norm_mix, norm_ffn, w_in, w_gk2, b_gk, q_gain, k_gain,
              gla_gain, w_out, w_router_grp, b_router_grp, w_router_exp, b_router_exp,
              w_gate, w_up, w_down):
    c_act = jax.nn.silu(c)
    for l in range(DEPTH):
        mod = (c_act @ w_ada[l] + b_ada[l])[:, None, :]
        shift_m, scale_m, gate_m, shift_f, scale_f, gate_f = jnp.split(mod, N_MOD, axis=-1)
        h = rms_norm(x, norm_mix[l]) * (1.0 + scale_m) + shift_m
        x = x + gate_m * hybrid_mixer(h, w_in[l], w_gk2[l], b_gk[l], q_gain[l], k_gain[l],
                                      gla_gain[l], w_out[l])
        h = rms_norm(x, norm_ffn[l]) * (1.0 + scale_f) + shift_f
        x = x + gate_f * hierarchical_moe(h, w_router_grp[l], b_router_grp[l], w_router_exp[l],
                                          b_router_exp[l], w_gate[l], w_up[l], w_down[l])
    return x
```

```python
import functools

import jax
import jax.numpy as jnp
from jax import lax
from jax.experimental import pallas as pl
from jax.experimental.pallas import tpu as pltpu

F32 = jnp.float32
BF16 = jnp.bfloat16

D_MODEL = 1024
SB_HEADS = 8
SB_HEAD_DIM = 64
SB_WIDTH = 512
GLA_HEADS = 4
GLA_DK = 64
GLA_DV = 128
GLA_K_WIDTH = 256
GLA_V_WIDTH = 512
GLA_RANK = 16
GLA_GATE_NORM = 16.0
GLA_CHUNK = 64
N_GROUPS = 4
EXPERTS_PER_GROUP = 8
N_EXPERTS = N_GROUPS * EXPERTS_PER_GROUP
D_EXPERT = 256
N_MOD = 6
EPS = 1e-6

LANES = 128
VMEM_LIMIT = 56 * 1024 * 1024

_C_Q, _C_K, _C_V = 0, 512, 1024
_C_GQ, _C_GK, _C_GV, _C_GR, _C_LR = 1536, 1792, 2048, 2560, 3072
D_IN_PAD = 3200

TM_PROJ = 512
TM_OUT = 512
TM_MOE = 1024
SB_BQ = 256
SB_BK = 128
GLA_TS = 512
SB_SKIP_THRESH = 110.0


def _params(sem):
    return pltpu.CompilerParams(dimension_semantics=sem, vmem_limit_bytes=VMEM_LIMIT)


def _split_hi_lo(x):
    hi = x.astype(BF16)
    lo = (x - hi.astype(F32)).astype(BF16)
    return hi, lo


def _softplus(z):
    return jnp.maximum(z, 0.0) + jnp.log(1.0 + jnp.exp(-jnp.abs(z)))


def _ada_kernel(c_ref, w_ref, b_ref, o_ref):
    c = c_ref[...]
    ca = (c * (1.0 / (1.0 + jnp.exp(-c)))).astype(BF16)
    o_ref[...] = jnp.dot(ca, w_ref[...].astype(BF16), preferred_element_type=F32) + b_ref[...]


def _ada_call(c, w_ada, b_ada):
    L, D, _ = w_ada.shape
    B = c.shape[0]
    return pl.pallas_call(
        _ada_kernel,
        grid=(L, N_MOD),
        in_specs=[
            pl.BlockSpec((B, D), lambda l, j: (0, 0)),
            pl.BlockSpec((None, D, D), lambda l, j: (l, 0, j)),
            pl.BlockSpec((None, 1, D), lambda l, j: (l, 0, j)),
        ],
        out_specs=pl.BlockSpec((None, B, D), lambda l, j: (l, 0, j)),
        out_shape=jax.ShapeDtypeStruct((L, B, N_MOD * D), F32),
        compiler_params=_params(("parallel", "parallel")),
        name="ada_mod",
    )(c, w_ada, b_ada.reshape(L, 1, N_MOD * D))


def _proj_kernel(x_ref, mod_ref, g_ref, w_ref, qg_ref, kg_ref, wgk_ref, bgk_ref,
                 q_ref, k_ref, v_ref, gq_ref, gk_ref, gv_ref, gr_ref, lg_ref):
    x = x_ref[...]
    shift = mod_ref[0:1, :]
    scale = mod_ref[1:2, :]
    y = x * lax.rsqrt(jnp.mean(x * x, axis=-1, keepdims=True) + EPS) * g_ref[...]
    h = (y * (1.0 + scale) + shift).astype(BF16)

    def proj(c0, width):
        return jnp.dot(h, w_ref[:, c0:c0 + width], preferred_element_type=F32)

    r = lax.broadcasted_iota(jnp.int32, (SB_WIDTH, SB_WIDTH), 0) // SB_HEAD_DIM
    c = lax.broadcasted_iota(jnp.int32, (SB_WIDTH, SB_WIDTH), 1) // SB_HEAD_DIM
    avg = jnp.where(r == c, 1.0 / SB_HEAD_DIM, 0.0).astype(BF16)

    def head_norm(t, gain):
        ms = jnp.dot((t * t).astype(BF16), avg, preferred_element_type=F32)
        return t * lax.rsqrt(ms + EPS) * gain

    q = head_norm(proj(_C_Q, SB_WIDTH), qg_ref[...])
    q_ref[...] = (q * (SB_HEAD_DIM ** -0.5)).astype(BF16)
    k_ref[...] = head_norm(proj(_C_K, SB_WIDTH), kg_ref[...]).astype(BF16)
    v_ref[...] = proj(_C_V, SB_WIDTH).astype(BF16)
    gq_ref[...] = proj(_C_GQ, GLA_K_WIDTH).astype(BF16)
    gk_ref[...] = proj(_C_GK, GLA_K_WIDTH).astype(BF16)
    gv_ref[...] = proj(_C_GV, GLA_V_WIDTH).astype(BF16)
    gr_ref[...] = proj(_C_GR, GLA_V_WIDTH).astype(BF16)
    lr = proj(_C_LR, LANES).astype(BF16)
    pre = jnp.dot(lr, wgk_ref[...], preferred_element_type=F32) + bgk_ref[...]
    lg_ref[...] = (jnp.minimum(pre, 0.0) - jnp.log(1.0 + jnp.exp(-jnp.abs(pre)))) * (1.0 / GLA_GATE_NORM)


def _proj_call(x, mod, g, w_in, q_gain, k_gain, w_gk2, b_gk, seq):
    N, D = x.shape
    tm = TM_PROJ
    per_b = seq // tm
    row = lambda i: (i, 0)
    const = lambda i: (0, 0)
    widths = [(SB_WIDTH, BF16)] * 3 + [(GLA_K_WIDTH, BF16)] * 2 + [(GLA_V_WIDTH, BF16)] * 2 + [(GLA_K_WIDTH, F32)]
    return pl.pallas_call(
        _proj_kernel,
        grid=(N // tm,),
        in_specs=[
            pl.BlockSpec((tm, D), row),
            pl.BlockSpec((None, N_MOD, D), lambda i: (i // per_b, 0, 0)),
            pl.BlockSpec((1, D), const),
            pl.BlockSpec((D, D_IN_PAD), const),
            pl.BlockSpec((1, SB_WIDTH), const),
            pl.BlockSpec((1, SB_WIDTH), const),
            pl.BlockSpec((LANES, GLA_K_WIDTH), const),
            pl.BlockSpec((1, GLA_K_WIDTH), const),
        ],
        out_specs=[pl.BlockSpec((tm, w), row) for w, _ in widths],
        out_shape=[jax.ShapeDtypeStruct((N, w), dt) for w, dt in widths],
        compiler_params=_params(("parallel",)),
        name="in_proj",
    )(x, mod, g, w_in, q_gain, k_gain, w_gk2, b_gk)


def _sb_kernel(q_ref, k_ref, v_ref, o_ref):
    i = pl.program_id(2)
    bq, bk = SB_BQ, SB_BK
    nsub = bq // bk
    lane = lax.broadcasted_iota(jnp.int32, (1, LANES), 1)
    q = q_ref[...]

    rr = lax.broadcasted_iota(jnp.int32, (2 * bk, bk + LANES), 0) % bk
    cc = lax.broadcasted_iota(jnp.int32, (2 * bk, bk + LANES), 1)
    uu = jnp.where((cc >= bk) | (rr >= cc), 1.0, 0.0).astype(BF16)

    t_loc = lax.broadcasted_iota(jnp.int32, (bq, bk), 0)
    s_loc = lax.broadcasted_iota(jnp.int32, (bq, bk), 1)

    def block(k_start, carry, mask):
        c, acc, qm = carry
        kb = k_ref[pl.ds(k_start, bk), :]
        vb = v_ref[pl.ds(k_start, bk), :]
        z = lax.dot_general(qm, kb, (((1,), (1,)), ((), ())), preferred_element_type=F32)
        sp = _softplus(z)
        lf = sp if mask is None else jnp.where(mask, sp, 0.0)
        hi, lo = _split_hi_lo(lf)
        t = jnp.dot(jnp.concatenate([hi, lo], axis=1), uu, preferred_element_type=F32)
        incl = t[:, :bk]
        tot = t[:, bk:]
        logw = z - sp + lf - incl - c
        w = jnp.exp(logw)
        if mask is not None:
            w = jnp.where(mask, w, 0.0)
        acc = acc + jnp.dot(w.astype(BF16), vb, preferred_element_type=F32)
        return c + tot, acc, qm

    outs = []
    for hh in range(2):
        in_head = (lane >= hh * SB_HEAD_DIM) & (lane < (hh + 1) * SB_HEAD_DIM)
        qm = jnp.where(in_head, q, jnp.zeros_like(q))
        carry = (jnp.zeros((bq, LANES), F32), jnp.zeros((bq, LANES), F32), qm)
        for r in reversed(range(nsub)):
            mask = (s_loc + r * bk) < t_loc
            k_start = pl.multiple_of(i * bq + r * bk, bk)
            carry = block(k_start, carry, mask)

        def cond(state):
            j, c, _, _ = state
            return jnp.logical_and(j >= 0, jnp.min(c) < SB_SKIP_THRESH)

        def body(state):
            j, c, acc, qm_ = state
            c, acc, qm_ = block(pl.multiple_of(j * bk, bk), (c, acc, qm_), None)
            return j - 1, c, acc, qm_

        _, _, acc, _ = lax.while_loop(cond, body, (i * nsub - 1,) + carry)
        outs.append(acc)
    o_ref[...] = jnp.where(lane < SB_HEAD_DIM, outs[0], outs[1]).astype(BF16)


def _sb_call(q, k, v, batch, seq):
    N = q.shape[0]
    nq = seq // SB_BQ
    npair = SB_WIDTH // LANES
    return pl.pallas_call(
        _sb_kernel,
        grid=(batch, npair, nq),
        in_specs=[
            pl.BlockSpec((SB_BQ, LANES), lambda b, p, i: (b * nq + i, p)),
            pl.BlockSpec((seq, LANES), lambda b, p, i: (b, p)),
            pl.BlockSpec((seq, LANES), lambda b, p, i: (b, p)),
        ],
        out_specs=pl.BlockSpec((SB_BQ, LANES), lambda b, p, i: (b * nq + i, p)),
        out_shape=jax.ShapeDtypeStruct((N, SB_WIDTH), BF16),
        compiler_params=_params(("parallel", "parallel", "arbitrary")),
        name="sb_attn",
    )(q, k, v)


def _gla_kernel(gq_ref, gk_ref, gv_ref, gr_ref, lg_ref, gain_ref, o_ref, st_ref):
    C = GLA_CHUNK

    @pl.when(pl.program_id(1) == 0)
    def _():
        st_ref[...] = jnp.zeros_like(st_ref)

    rows = lax.broadcasted_iota(jnp.int32, (C, C), 0)
    cols = lax.broadcasted_iota(jnp.int32, (C, C), 1)
    causal = rows >= cols
    low = jnp.where(causal, 1.0, 0.0).astype(BF16)
    low2 = jnp.concatenate([low, low], axis=1)
    lane = lax.broadcasted_iota(jnp.int32, (1, GLA_K_WIDTH), 1)
    gain = gain_ref[...]

    def chunk(ci, carry):
        r0 = pl.multiple_of(ci * C, C)
        lg = lg_ref[pl.ds(r0, C), :]
        hi, lo = _split_hi_lo(lg)
        b = jnp.dot(low2, jnp.concatenate([hi, lo], axis=0), preferred_element_type=F32)
        b_last = b[C - 1:C, :]
        gq = gq_ref[pl.ds(r0, C), :].astype(F32)
        gk = gk_ref[pl.ds(r0, C), :].astype(F32)
        q_dec = gq * jnp.exp(b) * (GLA_DK ** -0.5)
        k_inv = (gk * jnp.exp(-b)).astype(BF16)
        k_dec = gk * jnp.exp(b_last - b)
        decay = jnp.exp(b_last)
        st = st_ref[...]
        st_b = st.astype(BF16)
        kv = jnp.zeros_like(st)
        for h in range(GLA_HEADS):
            in_head = (lane >= h * GLA_DK) & (lane < (h + 1) * GLA_DK)
            qh = jnp.where(in_head, q_dec, 0.0).astype(BF16)
            kh = jnp.where(in_head, k_dec, 0.0).astype(BF16)
            vh = gv_ref[pl.ds(r0, C), h * GLA_DV:(h + 1) * GLA_DV]
            s = lax.dot_general(qh, k_inv, (((1,), (1,)), ((), ())), preferred_element_type=F32)
            s = jnp.where(causal, s, 0.0).astype(BF16)
            o = jnp.dot(s, vh, preferred_element_type=F32)
            o = o + lax.dot_general(qh, st_b, (((1,), (1,)), ((), ())), preferred_element_type=F32)
            y = o * lax.rsqrt(jnp.mean(o * o, axis=-1, keepdims=True) + EPS) * gain
            g = gr_ref[pl.ds(r0, C), h * GLA_DV:(h + 1) * GLA_DV].astype(F32)
            y = y * (g * (1.0 / (1.0 + jnp.exp(-g))))
            o_ref[pl.ds(r0, C), h * GLA_DV:(h + 1) * GLA_DV] = y.astype(BF16)
            vt = vh.astype(F32).T.astype(BF16)
            kv = kv + jnp.dot(vt, kh, preferred_element_type=F32)
        st_ref[...] = st * decay + kv
        return carry

    lax.fori_loop(0, GLA_TS // C, chunk, 0)


def _gla_call(gq, gk, gv, gr, lg, gain, batch, seq):
    N = gq.shape[0]
    nt = seq // GLA_TS
    row = lambda b, t: (b * nt + t, 0)
    return pl.pallas_call(
        _gla_kernel,
        grid=(batch, nt),
        in_specs=[
            pl.BlockSpec((GLA_TS, GLA_K_WIDTH), row),
            pl.BlockSpec((GLA_TS, GLA_K_WIDTH), row),
            pl.BlockSpec((GLA_TS, GLA_V_WIDTH), row),
            pl.BlockSpec((GLA_TS, GLA_V_WIDTH), row),
            pl.BlockSpec((GLA_TS, GLA_K_WIDTH), row),
            pl.BlockSpec((1, GLA_DV), lambda b, t: (0, 0)),
        ],
        out_specs=pl.BlockSpec((GLA_TS, GLA_V_WIDTH), row),
        out_shape=jax.ShapeDtypeStruct((N, GLA_V_WIDTH), BF16),
        scratch_shapes=[pltpu.VMEM((GLA_DV, GLA_K_WIDTH), F32)],
        compiler_params=_params(("parallel", "arbitrary")),
        name="gla",
    )(gq, gk, gv, gr, lg, gain)


def _out_kernel(a_ref, b_ref, x_ref, mod_ref, g_ref, wa_ref, wb_ref, wrh_ref, wrl_ref, br_ref,
                x1_ref, h2_ref, comb_ref):
    gate_m = mod_ref[2:3, :]
    shift_f = mod_ref[3:4, :]
    scale_f = mod_ref[4:5, :]
    y = jnp.dot(a_ref[...], wa_ref[...], preferred_element_type=F32)
    y = y + jnp.dot(b_ref[...], wb_ref[...], preferred_element_type=F32)
    x1 = x_ref[...] + gate_m * y
    x1_ref[...] = x1
    n = x1 * lax.rsqrt(jnp.mean(x1 * x1, axis=-1, keepdims=True) + EPS) * g_ref[...]
    h2 = n * (1.0 + scale_f) + shift_f
    h_hi, h_lo = _split_hi_lo(h2)
    h2_ref[...] = h_hi

    wr_hi = wrh_ref[...]
    logits = jnp.dot(h_hi, wr_hi, preferred_element_type=F32)
    logits = logits + jnp.dot(h_lo, wr_hi, preferred_element_type=F32)
    logits = logits + jnp.dot(h_hi, wrl_ref[...], preferred_element_type=F32)
    logits = logits + br_ref[...]

    lane = lax.broadcasted_iota(jnp.int32, logits.shape, 1)
    neg = jnp.float32(-jnp.inf)
    big = jnp.int32(1 << 20)
    is_grp = (lane >= N_EXPERTS) & (lane < N_EXPERTS + N_GROUPS)
    glog = jnp.where(is_grp, logits, neg)
    gmax = jnp.max(glog, axis=-1, keepdims=True)
    gidx = jnp.min(jnp.where(glog == gmax, lane - N_EXPERTS, big), axis=-1, keepdims=True)
    grp_w = 1.0 / jnp.sum(jnp.where(is_grp, jnp.exp(glog - gmax), 0.0), axis=-1, keepdims=True)

    lo_lane = gidx * EXPERTS_PER_GROUP
    in_grp = (lane >= lo_lane) & (lane < lo_lane + EXPERTS_PER_GROUP)
    elog = jnp.where(in_grp, logits, neg)
    m1 = jnp.max(elog, axis=-1, keepdims=True)
    i1 = jnp.min(jnp.where(elog == m1, lane, big), axis=-1, keepdims=True)
    elog2 = jnp.where(lane == i1, neg, elog)
    m2 = jnp.max(elog2, axis=-1, keepdims=True)
    i2 = jnp.min(jnp.where(elog2 == m2, lane, big), axis=-1, keepdims=True)
    e21 = jnp.exp(m2 - m1)
    p1 = 1.0 / (1.0 + e21)
    p2 = e21 * p1
    comb_ref[...] = jnp.where(lane == i1, p1 * grp_w, 0.0) + jnp.where(lane == i2, p2 * grp_w, 0.0)


def _out_call(oa, ob, x, mod, g, wa, wb, wr_hi, wr_lo, br, seq):
    N, D = x.shape
    tm = TM_OUT
    per_b = seq // tm
    row = lambda i: (i, 0)
    const = lambda i: (0, 0)
    return pl.pallas_call(
        _out_kernel,
        grid=(N // tm,),
        in_specs=[
            pl.BlockSpec((tm, SB_WIDTH), row),
            pl.BlockSpec((tm, GLA_V_WIDTH), row),
            pl.BlockSpec((tm, D), row),
            pl.BlockSpec((None, N_MOD, D), lambda i: (i // per_b, 0, 0)),
            pl.BlockSpec((1, D), const),
            pl.BlockSpec((SB_WIDTH, D), const),
            pl.BlockSpec((GLA_V_WIDTH, D), const),
            pl.BlockSpec((D, LANES), const),
            pl.BlockSpec((D, LANES), const),
            pl.BlockSpec((1, LANES), const),
        ],
        out_specs=[pl.BlockSpec((tm, D), row), pl.BlockSpec((tm, D), row), pl.BlockSpec((tm, LANES), row)],
        out_shape=[jax.ShapeDtypeStruct((N, D), F32), jax.ShapeDtypeStruct((N, D), BF16),
                   jax.ShapeDtypeStruct((N, LANES), F32)],
        compiler_params=_params(("parallel",)),
        name="out_proj_router",
    )(oa, ob, x, mod, g, wa, wb, wr_hi, wr_lo, br)


def _moe_kernel(h_ref, comb_ref, x1_ref, mod_ref, wgu_ref, wd_ref, o_ref, acc_ref):
    e = pl.program_id(1)

    @pl.when(e == 0)
    def _():
        acc_ref[...] = jnp.zeros_like(acc_ref)

    comb = comb_ref[...]
    lane = lax.broadcasted_iota(jnp.int32, comb.shape, 1)
    cw = jnp.sum(jnp.where(lane == e, comb, 0.0), axis=-1, keepdims=True)
    gu = jnp.dot(h_ref[...], wgu_ref[...], preferred_element_type=F32)
    a = gu[:, :D_EXPERT]
    u = gu[:, D_EXPERT:]
    act = (a * (1.0 / (1.0 + jnp.exp(-a)))) * u * cw
    acc_ref[...] += jnp.dot(act.astype(BF16), wd_ref[...], preferred_element_type=F32)

    @pl.when(e == pl.num_programs(1) - 1)
    def _():
        o_ref[...] = x1_ref[...] + mod_ref[5:6, :] * acc_ref[...]


def _moe_call(h2, comb, x1, mod, wgu, wd, seq):
    N, D = x1.shape
    tm = TM_MOE
    per_b = seq // tm
    row = lambda i, e: (i, 0)
    return pl.pallas_call(
        _moe_kernel,
        grid=(N // tm, N_EXPERTS),
        in_specs=[
            pl.BlockSpec((tm, D), row),
            pl.BlockSpec((tm, LANES), row),
            pl.BlockSpec((tm, D), row),
            pl.BlockSpec((None, N_MOD, D), lambda i, e: (i // per_b, 0, 0)),
            pl.BlockSpec((None, D, 2 * D_EXPERT), lambda i, e: (e, 0, 0)),
            pl.BlockSpec((None, D_EXPERT, D), lambda i, e: (e, 0, 0)),
        ],
        out_specs=pl.BlockSpec((tm, D), row),
        out_shape=jax.ShapeDtypeStruct((N, D), F32),
        scratch_shapes=[pltpu.VMEM((tm, D), F32)],
        compiler_params=_params(("parallel", "arbitrary")),
        name="moe_experts",
    )(h2, comb, x1, mod, wgu, wd)


def kernel(x, c, w_ada, b_ada, norm_mix, norm_ffn, w_in, w_gk2, b_gk, q_gain, k_gain, gla_gain, w_out,
           w_router_grp, b_router_grp, w_router_exp, b_router_exp, w_gate, w_up, w_down):
    B, S, D = x.shape
    L = w_ada.shape[0]
    N = B * S

    mod_all = _ada_call(c, w_ada, b_ada).reshape(L, B, N_MOD, D)

    w_in_p = jnp.pad(w_in, ((0, 0), (0, 0), (0, D_IN_PAD - w_in.shape[-1]))).astype(BF16)
    w_gk2_p = jnp.pad(w_gk2, ((0, 0), (0, LANES - GLA_RANK), (0, 0))).astype(BF16)
    qg = jnp.tile(q_gain, (1, SB_HEADS)).reshape(L, 1, SB_WIDTH)
    kg = jnp.tile(k_gain, (1, SB_HEADS)).reshape(L, 1, SB_WIDTH)
    w_out_b = w_out.astype(BF16)
    w_r = jnp.concatenate([w_router_exp, w_router_grp], axis=-1)
    w_r = jnp.pad(w_r, ((0, 0), (0, 0), (0, LANES - w_r.shape[-1])))
    w_r_hi = w_r.astype(BF16)
    w_r_lo = (w_r - w_r_hi.astype(F32)).astype(BF16)
    b_r = jnp.concatenate([b_router_exp, b_router_grp], axis=-1)
    b_r = jnp.pad(b_r, ((0, 0), (0, LANES - b_r.shape[-1]))).reshape(L, 1, LANES)
    wgu = jnp.concatenate([w_gate, w_up], axis=-1).astype(BF16).reshape(L, N_EXPERTS, D, 2 * D_EXPERT)
    wd = w_down.astype(BF16).reshape(L, N_EXPERTS, D_EXPERT, D)

    xf = x.reshape(N, D)
    for l in range(L):
        mod = mod_all[l]
        q, k, v, gq, gk, gv, gr, lg = _proj_call(
            xf, mod, norm_mix[l].reshape(1, D), w_in_p[l], qg[l], kg[l], w_gk2_p[l],
            b_gk[l].reshape(1, GLA_K_WIDTH), S)
        out_a = _sb_call(q, k, v, B, S)
        out_b = _gla_call(gq, gk, gv, gr, lg, gla_gain[l].reshape(1, GLA_DV), B, S)
        x1, h2, comb = _out_call(out_a, out_b, xf, mod, norm_ffn[l].reshape(1, D),
                                 w_out_b[l, :SB_WIDTH], w_out_b[l, SB_WIDTH:], w_r_hi[l], w_r_lo[l], b_r[l], S)
        xf = _moe_call(h2, comb, x1, mod, wgu[l], wd[l], S)
    return xf.reshape(B, S, D)
```

```python
import functools

import jax
import jax.numpy as jnp
from jax import lax
from jax.experimental import pallas as pl
from jax.experimental.pallas import tpu as pltpu
from jax.experimental.pallas import tpu_sc as plsc

F32 = jnp.float32
BF16 = jnp.bfloat16

D_MODEL = 1024
SB_HEADS = 8
SB_HEAD_DIM = 64
SB_WIDTH = 512
GLA_HEADS = 4
GLA_DK = 64
GLA_DV = 128
GLA_K_WIDTH = 256
GLA_V_WIDTH = 512
GLA_RANK = 16
GLA_GATE_NORM = 16.0
GLA_CHUNK = 64
N_GROUPS = 4
EXPERTS_PER_GROUP = 8
N_EXPERTS = N_GROUPS * EXPERTS_PER_GROUP
D_EXPERT = 256
N_MOD = 6
EPS = 1e-6

LANES = 128
SUBLANES = 8
VMEM_LIMIT = 56 * 1024 * 1024

_C_Q, _C_K, _C_V = 0, 512, 1024
_C_GQ, _C_GK, _C_GV, _C_GR, _C_LR = 1536, 1792, 2048, 2560, 3072
D_IN_PAD = 3200

TM_PROJ = 512
TM_OUT = 512
TM_MOE = 512
SC_WINDOW = 128
D_PACK = D_MODEL // 4
U32 = jnp.uint32
SB_BQ = 256
SB_BK = 128
GLA_TS = 512
LOG2E = 1.4426950408889634
SB_SKIP_THRESH = 160.0


def _params(sem):
    return pltpu.CompilerParams(dimension_semantics=sem, vmem_limit_bytes=VMEM_LIMIT)


def _split_hi_lo(x):
    hi = x.astype(BF16)
    lo = (x - hi.astype(F32)).astype(BF16)
    return hi, lo


def _pack_rows(x):
    r = pltpu.bitcast(x.astype(BF16).astype(F32), U32)
    out = []
    for half in range(2):
        c0 = half * 2 * D_PACK
        out.append(r[:, c0:c0 + D_PACK] | (r[:, c0 + D_PACK:c0 + 2 * D_PACK] >> 16))
    return out


def _unpack_rows(pa, pb):
    parts = []
    for p in (pa, pb):
        parts.append(pltpu.bitcast(p & U32(0xFFFF0000), F32))
        parts.append(pltpu.bitcast(p << 16, F32))
    return jnp.concatenate(parts, axis=1)


def _softplus2(z2):
    neg_abs = pltpu.bitcast(pltpu.bitcast(z2, jnp.uint32) | jnp.uint32(0x80000000), F32)
    return jnp.maximum(z2, 0.0) + jnp.log2(1.0 + jnp.exp2(neg_abs))


def _ada_kernel(c_ref, w_ref, b_ref, o_ref):
    c = c_ref[...]
    ca = (c * (1.0 / (1.0 + jnp.exp(-c)))).astype(BF16)
    o_ref[...] = jnp.dot(ca, w_ref[...].astype(BF16), preferred_element_type=F32) + b_ref[...]


def _ada_call(c, w_ada, b_ada):
    L, D, _ = w_ada.shape
    B = c.shape[0]
    return pl.pallas_call(
        _ada_kernel,
        grid=(L, N_MOD),
        in_specs=[
            pl.BlockSpec((B, D), lambda l, j: (0, 0)),
            pl.BlockSpec((None, D, D), lambda l, j: (l, 0, j)),
            pl.BlockSpec((None, 1, D), lambda l, j: (l, 0, j)),
        ],
        out_specs=pl.BlockSpec((None, B, D), lambda l, j: (l, 0, j)),
        out_shape=jax.ShapeDtypeStruct((L, B, N_MOD * D), F32),
        compiler_params=_params(("parallel", "parallel")),
        name="ada_mod",
    )(c, w_ada, b_ada.reshape(L, 1, N_MOD * D))


def _proj_kernel(x_ref, mod_ref, g_ref, w_ref, qg_ref, kg_ref, wgk_ref, bgk_ref,
                 q_ref, k_ref, v_ref, gq_ref, gk_ref, gv_ref, gr_ref, lg_ref):
    x = x_ref[...]
    shift = mod_ref[0:1, :]
    scale = mod_ref[1:2, :]
    y = x * lax.rsqrt(jnp.mean(x * x, axis=-1, keepdims=True) + EPS) * g_ref[...]
    h = (y * (1.0 + scale) + shift).astype(BF16)

    def proj(c0, width):
        return jnp.dot(h, w_ref[:, c0:c0 + width], preferred_element_type=F32)

    r = lax.broadcasted_iota(jnp.int32, (SB_WIDTH, SB_WIDTH), 0) // SB_HEAD_DIM
    c = lax.broadcasted_iota(jnp.int32, (SB_WIDTH, SB_WIDTH), 1) // SB_HEAD_DIM
    avg = jnp.where(r == c, 1.0 / SB_HEAD_DIM, 0.0).astype(BF16)

    def head_norm(t, gain):
        ms = jnp.dot((t * t).astype(BF16), avg, preferred_element_type=F32)
        return t * lax.rsqrt(ms + EPS) * gain

    q = head_norm(proj(_C_Q, SB_WIDTH), qg_ref[...])
    q_ref[...] = (q * (SB_HEAD_DIM ** -0.5 * LOG2E)).astype(BF16)
    k_ref[...] = head_norm(proj(_C_K, SB_WIDTH), kg_ref[...]).astype(BF16)
    v_ref[...] = proj(_C_V, SB_WIDTH).astype(BF16)
    gq_ref[...] = proj(_C_GQ, GLA_K_WIDTH).astype(BF16)
    gk_ref[...] = proj(_C_GK, GLA_K_WIDTH).astype(BF16)
    gv_ref[...] = proj(_C_GV, GLA_V_WIDTH).astype(BF16)
    gr_ref[...] = proj(_C_GR, GLA_V_WIDTH).astype(BF16)
    lr = proj(_C_LR, LANES).astype(BF16)
    pre = jnp.dot(lr, wgk_ref[...], preferred_element_type=F32) + bgk_ref[...]
    lg_ref[...] = (jnp.minimum(pre, 0.0) - jnp.log(1.0 + jnp.exp(-jnp.abs(pre)))) * (1.0 / GLA_GATE_NORM)


def _proj_call(x, mod, g, w_in, q_gain, k_gain, w_gk2, b_gk, seq):
    N, D = x.shape
    tm = TM_PROJ
    per_b = seq // tm
    row = lambda i: (i, 0)
    const = lambda i: (0, 0)
    widths = [(SB_WIDTH, BF16)] * 3 + [(GLA_K_WIDTH, BF16)] * 2 + [(GLA_V_WIDTH, BF16)] * 2 + [(GLA_K_WIDTH, F32)]
    return pl.pallas_call(
        _proj_kernel,
        grid=(N // tm,),
        in_specs=[
            pl.BlockSpec((tm, D), row),
            pl.BlockSpec((None, N_MOD, D), lambda i: (i // per_b, 0, 0)),
            pl.BlockSpec((1, D), const),
            pl.BlockSpec((D, D_IN_PAD), const),
            pl.BlockSpec((1, SB_WIDTH), const),
            pl.BlockSpec((1, SB_WIDTH), const),
            pl.BlockSpec((LANES, GLA_K_WIDTH), const),
            pl.BlockSpec((1, GLA_K_WIDTH), const),
        ],
        out_specs=[pl.BlockSpec((tm, w), row) for w, _ in widths],
        out_shape=[jax.ShapeDtypeStruct((N, w), dt) for w, dt in widths],
        compiler_params=_params(("parallel",)),
        name="in_proj",
    )(x, mod, g, w_in, q_gain, k_gain, w_gk2, b_gk)


def _sb_kernel(q_ref, k_ref, v_ref, o_ref):
    i = pl.program_id(2)
    bq, bk = SB_BQ, SB_BK
    nsub = bq // bk
    lane = lax.broadcasted_iota(jnp.int32, (1, LANES), 1)
    q = q_ref[...]
    qms = []
    for hh in range(2):
        in_head = (lane >= hh * SB_HEAD_DIM) & (lane < (hh + 1) * SB_HEAD_DIM)
        qms.append(jnp.where(in_head, q, jnp.zeros_like(q)))

    rr = lax.broadcasted_iota(jnp.int32, (2 * bk, bk + LANES), 0) % bk
    cc = lax.broadcasted_iota(jnp.int32, (2 * bk, bk + LANES), 1)
    uu = jnp.where((cc >= bk) | (rr >= cc), 1.0, 0.0).astype(BF16)

    t_loc = lax.broadcasted_iota(jnp.int32, (bq, bq), 0)
    s_loc = lax.broadcasted_iota(jnp.int32, (bq, bq), 1)
    diag_mask = s_loc < t_loc

    def step(k_start, cs, accs, mask):
        kb = k_ref[pl.ds(k_start, bq), :]
        vb = v_ref[pl.ds(k_start, bq), :]
        new_cs, new_accs = [], []
        for hh in range(2):
            z = lax.dot_general(qms[hh], kb, (((1,), (1,)), ((), ())), preferred_element_type=F32)
            sp = _softplus2(z)
            if mask is None:
                lf, g = sp, z
            else:
                lf = jnp.where(mask, sp, 0.0)
                g = z - sp + lf
            c = cs[hh]
            ws = [None] * nsub
            for r in reversed(range(nsub)):
                hi, lo = _split_hi_lo(lf[:, r * bk:(r + 1) * bk])
                t = jnp.dot(jnp.concatenate([hi, lo], axis=1), uu, preferred_element_type=F32)
                ws[r] = jnp.exp2(g[:, r * bk:(r + 1) * bk] - t[:, :bk] - c)
                c = c + t[:, bk:]
            w = jnp.concatenate(ws, axis=1)
            if mask is not None:
                w = jnp.where(mask, w, 0.0)
            new_accs.append(accs[hh] + jnp.dot(w.astype(BF16), vb, preferred_element_type=F32))
            new_cs.append(c)
        return new_cs, new_accs

    zero = jnp.zeros((bq, LANES), F32)

    def diag_only():
        cs, accs = step(pl.multiple_of(i * bq, bq), [zero, zero], [zero, zero], diag_mask)
        return cs[0], cs[1], accs[0], accs[1]

    def diag_and_previous():
        cs, accs = step(pl.multiple_of(i * bq, bq), [zero, zero], [zero, zero], diag_mask)
        cs, accs = step(pl.multiple_of((i - 1) * bq, bq), cs, accs, None)
        return cs[0], cs[1], accs[0], accs[1]

    c0, c1, a0, a1 = lax.cond(i == 0, diag_only, diag_and_previous)
    cs, accs = [c0, c1], [a0, a1]

    def cond(state):
        j, c0, c1, _, _ = state
        return jnp.logical_and(j >= 0, jnp.min(jnp.minimum(c0, c1)) < SB_SKIP_THRESH)

    def body(state):
        j, c0, c1, a0, a1 = state
        (c0, c1), (a0, a1) = step(pl.multiple_of(j * bq, bq), [c0, c1], [a0, a1], None)
        return j - 1, c0, c1, a0, a1

    _, _, _, a0, a1 = lax.while_loop(cond, body, (i - 2, cs[0], cs[1], accs[0], accs[1]))
    o_ref[...] = jnp.where(lane < SB_HEAD_DIM, a0, a1).astype(BF16)


def _sb_call(q, k, v, batch, seq):
    N = q.shape[0]
    nq = seq // SB_BQ
    npair = SB_WIDTH // LANES
    return pl.pallas_call(
        _sb_kernel,
        grid=(batch, npair, nq),
        in_specs=[
            pl.BlockSpec((SB_BQ, LANES), lambda b, p, i: (b * nq + i, p)),
            pl.BlockSpec((seq, LANES), lambda b, p, i: (b, p)),
            pl.BlockSpec((seq, LANES), lambda b, p, i: (b, p)),
        ],
        out_specs=pl.BlockSpec((SB_BQ, LANES), lambda b, p, i: (b * nq + i, p)),
        out_shape=jax.ShapeDtypeStruct((N, SB_WIDTH), BF16),
        compiler_params=_params(("parallel", "parallel", "arbitrary")),
        name="sb_attn",
    )(q, k, v)


def _gla_kernel(gq_ref, gk_ref, gv_ref, gr_ref, lg_ref, gain_ref, o_ref, st_ref):
    C = GLA_CHUNK

    @pl.when(pl.program_id(1) == 0)
    def _():
        st_ref[...] = jnp.zeros_like(st_ref)

    rows = lax.broadcasted_iota(jnp.int32, (C, C), 0)
    cols = lax.broadcasted_iota(jnp.int32, (C, C), 1)
    causal = rows >= cols
    low = jnp.where(causal, 1.0, 0.0).astype(BF16)
    low2 = jnp.concatenate([low, low], axis=1)
    lane = lax.broadcasted_iota(jnp.int32, (1, GLA_K_WIDTH), 1)
    gain = gain_ref[...]

    def chunk(ci, carry):
        r0 = pl.multiple_of(ci * C, C)
        lg = lg_ref[pl.ds(r0, C), :]
        hi, lo = _split_hi_lo(lg)
        b = jnp.dot(low2, jnp.concatenate([hi, lo], axis=0), preferred_element_type=F32)
        b_last = b[C - 1:C, :]
        gq = gq_ref[pl.ds(r0, C), :].astype(F32)
        gk = gk_ref[pl.ds(r0, C), :].astype(F32)
        q_dec = gq * jnp.exp(b) * (GLA_DK ** -0.5)
        k_inv = (gk * jnp.exp(-b)).astype(BF16)
        k_dec = gk * jnp.exp(b_last - b)
        decay = jnp.exp(b_last)
        st = st_ref[...]
        st_b = st.astype(BF16)
        kv = jnp.zeros_like(st)
        for h in range(GLA_HEADS):
            in_head = (lane >= h * GLA_DK) & (lane < (h + 1) * GLA_DK)
            qh = jnp.where(in_head, q_dec, 0.0).astype(BF16)
            kh = jnp.where(in_head, k_dec, 0.0).astype(BF16)
            vh = gv_ref[pl.ds(r0, C), h * GLA_DV:(h + 1) * GLA_DV]
            s = lax.dot_general(qh, k_inv, (((1,), (1,)), ((), ())), preferred_element_type=F32)
            s = jnp.where(causal, s, 0.0).astype(BF16)
            o = jnp.dot(s, vh, preferred_element_type=F32)
            o = o + lax.dot_general(qh, st_b, (((1,), (1,)), ((), ())), preferred_element_type=F32)
            y = o * lax.rsqrt(jnp.mean(o * o, axis=-1, keepdims=True) + EPS) * gain
            g = gr_ref[pl.ds(r0, C), h * GLA_DV:(h + 1) * GLA_DV].astype(F32)
            y = y * (g * (1.0 / (1.0 + jnp.exp(-g))))
            o_ref[pl.ds(r0, C), h * GLA_DV:(h + 1) * GLA_DV] = y.astype(BF16)
            vt = vh.astype(F32).T.astype(BF16)
            kv = kv + jnp.dot(vt, kh, preferred_element_type=F32)
        st_ref[...] = st * decay + kv
        return carry

    lax.fori_loop(0, GLA_TS // C, chunk, 0)


def _gla_call(gq, gk, gv, gr, lg, gain, batch, seq):
    N = gq.shape[0]
    nt = seq // GLA_TS
    row = lambda b, t: (b * nt + t, 0)
    return pl.pallas_call(
        _gla_kernel,
        grid=(batch, nt),
        in_specs=[
            pl.BlockSpec((GLA_TS, GLA_K_WIDTH), row),
            pl.BlockSpec((GLA_TS, GLA_K_WIDTH), row),
            pl.BlockSpec((GLA_TS, GLA_V_WIDTH), row),
            pl.BlockSpec((GLA_TS, GLA_V_WIDTH), row),
            pl.BlockSpec((GLA_TS, GLA_K_WIDTH), row),
            pl.BlockSpec((1, GLA_DV), lambda b, t: (0, 0)),
        ],
        out_specs=pl.BlockSpec((GLA_TS, GLA_V_WIDTH), row),
        out_shape=jax.ShapeDtypeStruct((N, GLA_V_WIDTH), BF16),
        scratch_shapes=[pltpu.VMEM((GLA_DV, GLA_K_WIDTH), F32)],
        compiler_params=_params(("parallel", "arbitrary")),
        name="gla",
    )(gq, gk, gv, gr, lg, gain)


def _out_kernel(a_ref, b_ref, x_ref, mod_ref, g_ref, wa_ref, wb_ref, wrh_ref, wrl_ref, br_ref,
                x1_ref, h2a_ref, h2b_ref, route_ref, cnt_ref, run_ref):
    @pl.when(pl.program_id(0) == 0)
    def _():
        run_ref[...] = jnp.zeros_like(run_ref)

    gate_m = mod_ref[2:3, :]
    shift_f = mod_ref[3:4, :]
    scale_f = mod_ref[4:5, :]
    y = jnp.dot(a_ref[...], wa_ref[...], preferred_element_type=F32)
    y = y + jnp.dot(b_ref[...], wb_ref[...], preferred_element_type=F32)
    x1 = x_ref[...] + gate_m * y
    x1_ref[...] = x1
    n = x1 * lax.rsqrt(jnp.mean(x1 * x1, axis=-1, keepdims=True) + EPS) * g_ref[...]
    h2 = n * (1.0 + scale_f) + shift_f
    h_hi, h_lo = _split_hi_lo(h2)
    h2a_ref[...], h2b_ref[...] = _pack_rows(h2)

    wr_hi = wrh_ref[...]
    logits = jnp.dot(h_hi, wr_hi, preferred_element_type=F32)
    logits = logits + jnp.dot(h_lo, wr_hi, preferred_element_type=F32)
    logits = logits + jnp.dot(h_hi, wrl_ref[...], preferred_element_type=F32)
    logits = logits + br_ref[...]

    lane = lax.broadcasted_iota(jnp.int32, logits.shape, 1)
    neg = jnp.float32(-jnp.inf)
    big = jnp.int32(1 << 20)
    is_grp = (lane >= N_EXPERTS) & (lane < N_EXPERTS + N_GROUPS)
    glog = jnp.where(is_grp, logits, neg)
    gmax = jnp.max(glog, axis=-1, keepdims=True)
    gidx = jnp.min(jnp.where(glog == gmax, lane - N_EXPERTS, big), axis=-1, keepdims=True)
    grp_w = 1.0 / jnp.sum(jnp.where(is_grp, jnp.exp(glog - gmax), 0.0), axis=-1, keepdims=True)

    lo_lane = gidx * EXPERTS_PER_GROUP
    in_grp = (lane >= lo_lane) & (lane < lo_lane + EXPERTS_PER_GROUP)
    elog = jnp.where(in_grp, logits, neg)
    m1 = jnp.max(elog, axis=-1, keepdims=True)
    i1 = jnp.min(jnp.where(elog == m1, lane, big), axis=-1, keepdims=True)
    elog2 = jnp.where(lane == i1, neg, elog)
    m2 = jnp.max(elog2, axis=-1, keepdims=True)
    i2 = jnp.min(jnp.where(elog2 == m2, lane, big), axis=-1, keepdims=True)
    e21 = jnp.exp(m2 - m1)
    p1 = 1.0 / (1.0 + e21)
    p2 = e21 * p1

    tm = logits.shape[0]
    sel1 = lane == i1
    sel2 = lane == i2
    onehot = jnp.where(sel1 | sel2, 1.0, 0.0)
    rr = lax.broadcasted_iota(jnp.int32, (tm, tm), 0)
    cc = lax.broadcasted_iota(jnp.int32, (tm, tm), 1)
    before = jnp.where(rr > cc, 1.0, 0.0).astype(BF16)
    seen = jnp.dot(before, onehot.astype(BF16), preferred_element_type=F32) + run_ref[0:1, :]
    rank1 = jnp.sum(jnp.where(sel1, seen, 0.0), axis=-1, keepdims=True)
    rank2 = jnp.sum(jnp.where(sel2, seen, 0.0), axis=-1, keepdims=True)
    run = run_ref[...] + jnp.sum(onehot, axis=0, keepdims=True)
    run_ref[...] = run
    cnt_ref[...] = run

    fields = (i1.astype(F32), i2.astype(F32), p1 * grp_w, p2 * grp_w, rank1, rank2)
    route = jnp.zeros_like(logits)
    for idx, val in enumerate(fields):
        route = jnp.where(lane == idx, val, route)
    route_ref[...] = route


def _out_call(oa, ob, x, mod, g, wa, wb, wr_hi, wr_lo, br, seq):
    N, D = x.shape
    tm = TM_OUT
    per_b = seq // tm
    row = lambda i: (i, 0)
    const = lambda i: (0, 0)
    return pl.pallas_call(
        _out_kernel,
        grid=(N // tm,),
        in_specs=[
            pl.BlockSpec((tm, SB_WIDTH), row),
            pl.BlockSpec((tm, GLA_V_WIDTH), row),
            pl.BlockSpec((tm, D), row),
            pl.BlockSpec((None, N_MOD, D), lambda i: (i // per_b, 0, 0)),
            pl.BlockSpec((1, D), const),
            pl.BlockSpec((SB_WIDTH, D), const),
            pl.BlockSpec((GLA_V_WIDTH, D), const),
            pl.BlockSpec((D, LANES), const),
            pl.BlockSpec((D, LANES), const),
            pl.BlockSpec((1, LANES), const),
        ],
        out_specs=[pl.BlockSpec((tm, D), row), pl.BlockSpec((tm, D_PACK), row), pl.BlockSpec((tm, D_PACK), row),
                   pl.BlockSpec((tm, LANES), row), pl.BlockSpec((SUBLANES, LANES), const)],
        out_shape=[jax.ShapeDtypeStruct((N, D), F32), jax.ShapeDtypeStruct((N, D_PACK), U32),
                   jax.ShapeDtypeStruct((N, D_PACK), U32), jax.ShapeDtypeStruct((N, LANES), F32),
                   jax.ShapeDtypeStruct((SUBLANES, LANES), F32)],
        scratch_shapes=[pltpu.VMEM((SUBLANES, LANES), F32)],
        compiler_params=_params(("arbitrary",)),
        name="out_proj_router",
    )(oa, ob, x, mod, g, wa, wb, wr_hi, wr_lo, br)


def _sc_mesh():
    return plsc.VectorSubcoreMesh(core_axis_name="c", subcore_axis_name="s")


def _sc_dispatch(halves, pos, n_rows):
    N, dh = halves[0].shape
    w = SC_WINDOW
    nblk = N // w
    out_type = [jax.ShapeDtypeStruct((n_rows, dh), h.dtype) for h in halves]

    @pl.kernel(out_type=out_type, mesh=_sc_mesh(), scratch_types=[], name="moe_dispatch")
    def run(xa_hbm, xb_hbm, i_hbm, oa_hbm, ob_hbm):
        for x_hbm, o_hbm in ((xa_hbm, oa_hbm), (xb_hbm, ob_hbm)):
            def body(x_vmem, i_vmem, o_hbm=o_hbm):
                pltpu.sync_copy(x_vmem, o_hbm.at[i_vmem.at[0]])

            pltpu.emit_pipeline(
                body,
                grid=(2 * nblk,),
                in_specs=[pl.BlockSpec((w, dh), lambda j: (j % nblk, 0)),
                          pl.BlockSpec((1, w), lambda j: (0, j))],
                out_specs=[],
                core_axis_name=("c", "s"),
                dimension_semantics=(pltpu.PARALLEL,),
            )(x_hbm, i_hbm)

    return run(halves[0], halves[1], pos)


def _sc_collect(halves, pos):
    dh = halves[0].shape[1]
    n_idx = pos.shape[1]
    w = SC_WINDOW
    out_type = [jax.ShapeDtypeStruct((n_idx, dh), h.dtype) for h in halves]

    @pl.kernel(out_type=out_type, mesh=_sc_mesh(), scratch_types=[], name="moe_collect")
    def run(ya_hbm, yb_hbm, i_hbm, oa_hbm, ob_hbm):
        for y_hbm, o_hbm in ((ya_hbm, oa_hbm), (yb_hbm, ob_hbm)):
            def body(i_vmem, o_vmem, y_hbm=y_hbm):
                pltpu.sync_copy(y_hbm.at[i_vmem.at[0]], o_vmem)

            pltpu.emit_pipeline(
                body,
                grid=(n_idx // w,),
                in_specs=[pl.BlockSpec((1, w), lambda j: (0, j))],
                out_specs=[pl.BlockSpec((w, dh), lambda j: (j, 0))],
                core_axis_name=("c", "s"),
                dimension_semantics=(pltpu.PARALLEL,),
            )(i_hbm, o_hbm)

    return run(halves[0], halves[1], pos)


def _experts_kernel(te_ref, na_ref, xa_ref, xb_ref, wgu_ref, wd_ref, oa_ref, ob_ref):
    del te_ref

    @pl.when(pl.program_id(0) < na_ref[0])
    def _():
        x = _unpack_rows(xa_ref[...], xb_ref[...]).astype(BF16)
        gu = jnp.dot(x, wgu_ref[...], preferred_element_type=F32)
        a = gu[:, :D_EXPERT]
        u = gu[:, D_EXPERT:]
        act = (a * (1.0 / (1.0 + jnp.exp(-a)))) * u
        y = jnp.dot(act.astype(BF16), wd_ref[...], preferred_element_type=F32)
        oa_ref[...], ob_ref[...] = _pack_rows(y)


def _experts_call(tile_expert, n_active, xs, wgu, wd):
    R, dh = xs[0].shape
    tm = TM_MOE
    rows = lambda i, te, na: (jnp.minimum(i, na[0] - 1), 0)
    grid_spec = pltpu.PrefetchScalarGridSpec(
        num_scalar_prefetch=2,
        grid=(R // tm,),
        in_specs=[
            pl.BlockSpec((tm, dh), rows),
            pl.BlockSpec((tm, dh), rows),
            pl.BlockSpec((None, D_MODEL, 2 * D_EXPERT), lambda i, te, na: (te[i], 0, 0)),
            pl.BlockSpec((None, D_EXPERT, D_MODEL), lambda i, te, na: (te[i], 0, 0)),
        ],
        out_specs=[pl.BlockSpec((tm, dh), rows), pl.BlockSpec((tm, dh), rows)],
    )
    return pl.pallas_call(
        _experts_kernel,
        grid_spec=grid_spec,
        out_shape=[jax.ShapeDtypeStruct((R, dh), U32)] * 2,
        compiler_params=_params(("arbitrary",)),
        name="moe_experts",
    )(tile_expert, n_active, xs[0], xs[1], wgu, wd)


def _combine_kernel(x1_ref, y1a_ref, y1b_ref, y2a_ref, y2b_ref, route_ref, mod_ref, o_ref):
    route = route_ref[...]
    lane = lax.broadcasted_iota(jnp.int32, route.shape, 1)
    w1 = jnp.sum(jnp.where(lane == 2, route, 0.0), axis=-1, keepdims=True)
    w2 = jnp.sum(jnp.where(lane == 3, route, 0.0), axis=-1, keepdims=True)
    y1 = _unpack_rows(y1a_ref[...], y1b_ref[...])
    y2 = _unpack_rows(y2a_ref[...], y2b_ref[...])
    o_ref[...] = x1_ref[...] + mod_ref[5:6, :] * (w1 * y1 + w2 * y2)


def _combine_call(x1, ys, route, mod, seq):
    N, D = x1.shape
    tm = TM_OUT
    per_b = seq // tm
    nblk = N // tm
    dh = ys[0].shape[1]
    row = lambda i: (i, 0)
    second = lambda i: (i + nblk, 0)
    return pl.pallas_call(
        _combine_kernel,
        grid=(nblk,),
        in_specs=[
            pl.BlockSpec((tm, D), row),
            pl.BlockSpec((tm, dh), row),
            pl.BlockSpec((tm, dh), row),
            pl.BlockSpec((tm, dh), second),
            pl.BlockSpec((tm, dh), second),
            pl.BlockSpec((tm, LANES), row),
            pl.BlockSpec((None, N_MOD, D), lambda i: (i // per_b, 0, 0)),
        ],
        out_specs=pl.BlockSpec((tm, D), row),
        out_shape=jax.ShapeDtypeStruct((N, D), F32),
        compiler_params=_params(("parallel",)),
        name="moe_combine",
    )(x1, ys[0], ys[1], ys[0], ys[1], route, mod)


def _routing_tables(route, counts, n_rows):
    tm = TM_MOE
    cnt = counts[0, :N_EXPERTS].astype(jnp.int32)
    padded = ((cnt + tm - 1) // tm) * tm
    ends = jnp.cumsum(padded)
    base = ends - padded
    ea = route[:, 0].astype(jnp.int32)
    eb = route[:, 1].astype(jnp.int32)
    pos_a = base[ea] + route[:, 4].astype(jnp.int32)
    pos_b = base[eb] + route[:, 5].astype(jnp.int32)
    pos = jnp.concatenate([pos_a, pos_b]).reshape(1, -1)
    n_active = (ends[-1] // tm).reshape(1)
    tile_start = jnp.arange(n_rows // tm, dtype=jnp.int32) * tm
    tile_start = jnp.minimum(tile_start, ends[-1] - tm)
    tile_expert = jnp.sum(tile_start[:, None] >= ends[None, :], axis=1).astype(jnp.int32)
    return pos, tile_expert, n_active


def kernel(x, c, w_ada, b_ada, norm_mix, norm_ffn, w_in, w_gk2, b_gk, q_gain, k_gain, gla_gain, w_out,
           w_router_grp, b_router_grp, w_router_exp, b_router_exp, w_gate, w_up, w_down):
    B, S, D = x.shape
    L = w_ada.shape[0]
    N = B * S

    mod_all = _ada_call(c, w_ada, b_ada).reshape(L, B, N_MOD, D)

    w_in_p = jnp.pad(w_in, ((0, 0), (0, 0), (0, D_IN_PAD - w_in.shape[-1]))).astype(BF16)
    w_gk2_p = jnp.pad(w_gk2, ((0, 0), (0, LANES - GLA_RANK), (0, 0))).astype(BF16)
    qg = jnp.tile(q_gain, (1, SB_HEADS)).reshape(L, 1, SB_WIDTH)
    kg = jnp.tile(k_gain, (1, SB_HEADS)).reshape(L, 1, SB_WIDTH)
    w_out_b = w_out.astype(BF16)
    w_r = jnp.concatenate([w_router_exp, w_router_grp], axis=-1)
    w_r = jnp.pad(w_r, ((0, 0), (0, 0), (0, LANES - w_r.shape[-1])))
    w_r_hi = w_r.astype(BF16)
    w_r_lo = (w_r - w_r_hi.astype(F32)).astype(BF16)
    b_r = jnp.concatenate([b_router_exp, b_router_grp], axis=-1)
    b_r = jnp.pad(b_r, ((0, 0), (0, LANES - b_r.shape[-1]))).reshape(L, 1, LANES)
    wgu = jnp.concatenate([w_gate, w_up], axis=-1).astype(BF16).reshape(L, N_EXPERTS, D, 2 * D_EXPERT)
    wd = w_down.astype(BF16).reshape(L, N_EXPERTS, D_EXPERT, D)

    n_rows = 2 * N + N_EXPERTS * TM_MOE
    xf = x.reshape(N, D)
    for l in range(L):
        mod = mod_all[l]
        q, k, v, gq, gk, gv, gr, lg = _proj_call(
            xf, mod, norm_mix[l].reshape(1, D), w_in_p[l], qg[l], kg[l], w_gk2_p[l],
            b_gk[l].reshape(1, GLA_K_WIDTH), S)
        out_a = _sb_call(q, k, v, B, S)
        out_b = _gla_call(gq, gk, gv, gr, lg, gla_gain[l].reshape(1, GLA_DV), B, S)
        x1, h2a, h2b, route, counts = _out_call(out_a, out_b, xf, mod, norm_ffn[l].reshape(1, D),
                                          w_out_b[l, :SB_WIDTH], w_out_b[l, SB_WIDTH:], w_r_hi[l], w_r_lo[l],
                                          b_r[l], S)
        pos, tile_expert, n_active = _routing_tables(route, counts, n_rows)
        xs = _sc_dispatch((h2a, h2b), pos, n_rows)
        ys = _experts_call(tile_expert, n_active, xs, wgu[l], wd[l])
        yt = _sc_collect(ys, pos)
        xf = _combine_call(x1, yt, route, mod, S)
    return xf.reshape(B, S, D)
```

```python
import functools

import jax
import jax.numpy as jnp
from jax import lax
from jax.experimental import pallas as pl
from jax.experimental.pallas import tpu as pltpu
from jax.experimental.pallas import tpu_sc as plsc

F32 = jnp.float32
BF16 = jnp.bfloat16

D_MODEL = 1024
SB_HEADS = 8
SB_HEAD_DIM = 64
SB_WIDTH = 512
GLA_HEADS = 4
GLA_DK = 64
GLA_DV = 128
GLA_K_WIDTH = 256
GLA_V_WIDTH = 512
GLA_RANK = 16
GLA_GATE_NORM = 16.0
GLA_CHUNK = 64
N_GROUPS = 4
EXPERTS_PER_GROUP = 8
N_EXPERTS = N_GROUPS * EXPERTS_PER_GROUP
D_EXPERT = 256
N_MOD = 6
EPS = 1e-6

LANES = 128
SUBLANES = 8
VMEM_LIMIT = 56 * 1024 * 1024

_C_Q, _C_K, _C_V = 0, 512, 1024
_C_GQ, _C_GK, _C_GV, _C_GR, _C_LR = 1536, 1792, 2048, 2560, 3072
D_IN_PAD = 3200

TM_PROJ = 512
TM_OUT = 512
TM_MOE = 512
SC_WINDOW = 128
D_PACK = D_MODEL // 4
U32 = jnp.uint32
SB_BQ = 256
SB_BK = 128
SB_PAIRS = 2
GLA_TS = 512
GLA_SUB = 256
LOG2E = 1.4426950408889634
SB_SKIP_THRESH = 160.0


def _params(sem):
    return pltpu.CompilerParams(dimension_semantics=sem, vmem_limit_bytes=VMEM_LIMIT)


def _split_hi_lo(x):
    hi = x.astype(BF16)
    lo = (x - hi.astype(F32)).astype(BF16)
    return hi, lo


def _pack_rows(x):
    r = pltpu.bitcast(x.astype(BF16).astype(F32), U32)
    out = []
    for half in range(2):
        c0 = half * 2 * D_PACK
        out.append(r[:, c0:c0 + D_PACK] | (r[:, c0 + D_PACK:c0 + 2 * D_PACK] >> 16))
    return out


def _unpack_rows(pa, pb):
    parts = []
    for p in (pa, pb):
        parts.append(pltpu.bitcast(p & U32(0xFFFF0000), F32))
        parts.append(pltpu.bitcast(p << 16, F32))
    return jnp.concatenate(parts, axis=1)


def _softplus2(z2):
    neg_abs = pltpu.bitcast(pltpu.bitcast(z2, jnp.uint32) | jnp.uint32(0x80000000), F32)
    return jnp.maximum(z2, 0.0) + jnp.log2(1.0 + jnp.exp2(neg_abs))


def _ada_kernel(c_ref, w_ref, b_ref, o_ref):
    c = c_ref[...]
    ca = (c * (1.0 / (1.0 + jnp.exp(-c)))).astype(BF16)
    o_ref[...] = jnp.dot(ca, w_ref[...].astype(BF16), preferred_element_type=F32) + b_ref[...]


def _ada_call(c, w_ada, b_ada):
    L, D, _ = w_ada.shape
    B = c.shape[0]
    return pl.pallas_call(
        _ada_kernel,
        grid=(L, N_MOD),
        in_specs=[
            pl.BlockSpec((B, D), lambda l, j: (0, 0)),
            pl.BlockSpec((None, D, D), lambda l, j: (l, 0, j)),
            pl.BlockSpec((None, 1, D), lambda l, j: (l, 0, j)),
        ],
        out_specs=pl.BlockSpec((None, B, D), lambda l, j: (l, 0, j)),
        out_shape=jax.ShapeDtypeStruct((L, B, N_MOD * D), F32),
        compiler_params=_params(("parallel", "parallel")),
        name="ada_mod",
    )(c, w_ada, b_ada.reshape(L, 1, N_MOD * D))


def _proj_kernel(x_ref, mod_ref, g_ref, w_ref, qg_ref, kg_ref, wgk_ref, bgk_ref,
                 q_ref, k_ref, v_ref, gq_ref, gk_ref, gv_ref, gr_ref, lg_ref):
    x = x_ref[...]
    shift = mod_ref[0:1, :]
    scale = mod_ref[1:2, :]
    y = x * lax.rsqrt(jnp.mean(x * x, axis=-1, keepdims=True) + EPS) * g_ref[...]
    h = (y * (1.0 + scale) + shift).astype(BF16)

    def proj(c0, width):
        return jnp.dot(h, w_ref[:, c0:c0 + width], preferred_element_type=F32)

    r = lax.broadcasted_iota(jnp.int32, (SB_WIDTH, SB_WIDTH), 0) // SB_HEAD_DIM
    c = lax.broadcasted_iota(jnp.int32, (SB_WIDTH, SB_WIDTH), 1) // SB_HEAD_DIM
    avg = jnp.where(r == c, 1.0 / SB_HEAD_DIM, 0.0).astype(BF16)

    def head_norm(t, gain):
        ms = jnp.dot((t * t).astype(BF16), avg, preferred_element_type=F32)
        return t * lax.rsqrt(ms + EPS) * gain

    q = head_norm(proj(_C_Q, SB_WIDTH), qg_ref[...])
    q_ref[...] = (q * (SB_HEAD_DIM ** -0.5 * LOG2E)).astype(BF16)
    k_ref[...] = head_norm(proj(_C_K, SB_WIDTH), kg_ref[...]).astype(BF16)
    v_ref[...] = proj(_C_V, SB_WIDTH).astype(BF16)
    gq_ref[...] = proj(_C_GQ, GLA_K_WIDTH).astype(BF16)
    gk_ref[...] = proj(_C_GK, GLA_K_WIDTH).astype(BF16)
    gv_ref[...] = proj(_C_GV, GLA_V_WIDTH).astype(BF16)
    gr_ref[...] = proj(_C_GR, GLA_V_WIDTH).astype(BF16)
    lr = proj(_C_LR, LANES).astype(BF16)
    pre = jnp.dot(lr, wgk_ref[...], preferred_element_type=F32) + bgk_ref[...]
    lg_ref[...] = (jnp.minimum(pre, 0.0) - jnp.log(1.0 + jnp.exp(-jnp.abs(pre)))) * (1.0 / GLA_GATE_NORM)


def _proj_call(x, mod, g, w_in, q_gain, k_gain, w_gk2, b_gk, seq):
    N, D = x.shape
    tm = TM_PROJ
    per_b = seq // tm
    row = lambda i: (i, 0)
    const = lambda i: (0, 0)
    widths = [(SB_WIDTH, BF16)] * 3 + [(GLA_K_WIDTH, BF16)] * 2 + [(GLA_V_WIDTH, BF16)] * 2 + [(GLA_K_WIDTH, F32)]
    return pl.pallas_call(
        _proj_kernel,
        grid=(N // tm,),
        in_specs=[
            pl.BlockSpec((tm, D), row),
            pl.BlockSpec((None, N_MOD, D), lambda i: (i // per_b, 0, 0)),
            pl.BlockSpec((1, D), const),
            pl.BlockSpec((D, D_IN_PAD), const),
            pl.BlockSpec((1, SB_WIDTH), const),
            pl.BlockSpec((1, SB_WIDTH), const),
            pl.BlockSpec((LANES, GLA_K_WIDTH), const),
            pl.BlockSpec((1, GLA_K_WIDTH), const),
        ],
        out_specs=[pl.BlockSpec((tm, w), row) for w, _ in widths],
        out_shape=[jax.ShapeDtypeStruct((N, w), dt) for w, dt in widths],
        compiler_params=_params(("parallel",)),
        name="in_proj",
    )(x, mod, g, w_in, q_gain, k_gain, w_gk2, b_gk)


def _sb_kernel(q_ref, k_ref, v_ref, o_ref):
    i = pl.program_id(2)
    bq, bk = SB_BQ, SB_BK
    nsub = bq // bk
    lane = lax.broadcasted_iota(jnp.int32, (1, LANES), 1)
    qms = []
    for pr in range(SB_PAIRS):
        q = q_ref[:, pr * LANES:(pr + 1) * LANES]
        for hh in range(2):
            in_head = (lane >= hh * SB_HEAD_DIM) & (lane < (hh + 1) * SB_HEAD_DIM)
            qms.append(jnp.where(in_head, q, jnp.zeros_like(q)))
    nh = len(qms)

    rr = lax.broadcasted_iota(jnp.int32, (2 * bk, bk + LANES), 0) % bk
    cc = lax.broadcasted_iota(jnp.int32, (2 * bk, bk + LANES), 1)
    uu = jnp.where((cc >= bk) | (rr >= cc), 1.0, 0.0).astype(BF16)

    t_loc = lax.broadcasted_iota(jnp.int32, (bq, bq), 0)
    s_loc = lax.broadcasted_iota(jnp.int32, (bq, bq), 1)
    diag_mask = s_loc < t_loc

    def step(k_start, cs, accs, mask):
        new_cs, new_accs = [], []
        for hh in range(nh):
            pr = hh // 2
            kb = k_ref[pl.ds(k_start, bq), pr * LANES:(pr + 1) * LANES]
            vb = v_ref[pl.ds(k_start, bq), pr * LANES:(pr + 1) * LANES]
            z = lax.dot_general(qms[hh], kb, (((1,), (1,)), ((), ())), preferred_element_type=F32)
            sp = _softplus2(z)
            if mask is None:
                lf, g = sp, z
            else:
                lf = jnp.where(mask, sp, 0.0)
                g = z - sp + lf
            c = cs[hh]
            ws = [None] * nsub
            for r in reversed(range(nsub)):
                hi, lo = _split_hi_lo(lf[:, r * bk:(r + 1) * bk])
                t = jnp.dot(jnp.concatenate([hi, lo], axis=1), uu, preferred_element_type=F32)
                ws[r] = jnp.exp2(g[:, r * bk:(r + 1) * bk] - t[:, :bk] - c)
                c = c + t[:, bk:]
            w = jnp.concatenate(ws, axis=1)
            if mask is not None:
                w = jnp.where(mask, w, 0.0)
            new_accs.append(accs[hh] + jnp.dot(w.astype(BF16), vb, preferred_element_type=F32))
            new_cs.append(c)
        return new_cs, new_accs

    zeros = [jnp.zeros((bq, LANES), F32)] * nh

    def diag_only():
        cs, accs = step(pl.multiple_of(i * bq, bq), zeros, zeros, diag_mask)
        return tuple(cs) + tuple(accs)

    def diag_and_previous():
        cs, accs = step(pl.multiple_of(i * bq, bq), zeros, zeros, diag_mask)
        cs, accs = step(pl.multiple_of((i - 1) * bq, bq), cs, accs, None)
        return tuple(cs) + tuple(accs)

    state0 = lax.cond(i == 0, diag_only, diag_and_previous)

    def cond(state):
        c_min = functools.reduce(jnp.minimum, state[1:1 + nh])
        return jnp.logical_and(state[0] >= 0, jnp.min(c_min) < SB_SKIP_THRESH)

    def body(state):
        j = state[0]
        cs, accs = step(pl.multiple_of(j * bq, bq), list(state[1:1 + nh]), list(state[1 + nh:]), None)
        return (j - 1,) + tuple(cs) + tuple(accs)

    final = lax.while_loop(cond, body, (i - 2,) + tuple(state0))
    accs = final[1 + nh:]
    for pr in range(SB_PAIRS):
        o_ref[:, pr * LANES:(pr + 1) * LANES] = jnp.where(
            lane < SB_HEAD_DIM, accs[2 * pr], accs[2 * pr + 1]).astype(BF16)


def _sb_call(q, k, v, batch, seq):
    N = q.shape[0]
    nq = seq // SB_BQ
    width = SB_PAIRS * LANES
    return pl.pallas_call(
        _sb_kernel,
        grid=(batch, SB_WIDTH // width, nq),
        in_specs=[
            pl.BlockSpec((SB_BQ, width), lambda b, p, i: (b * nq + i, p)),
            pl.BlockSpec((seq, width), lambda b, p, i: (b, p)),
            pl.BlockSpec((seq, width), lambda b, p, i: (b, p)),
        ],
        out_specs=pl.BlockSpec((SB_BQ, width), lambda b, p, i: (b * nq + i, p)),
        out_shape=jax.ShapeDtypeStruct((N, SB_WIDTH), BF16),
        compiler_params=_params(("parallel", "parallel", "arbitrary")),
        name="sb_attn",
    )(q, k, v)


def _gla_kernel(gq_ref, gk_ref, gv_ref, gr_ref, lg_ref, gain_ref, o_ref, st_ref):
    C, SUB, H = GLA_CHUNK, GLA_SUB, GLA_HEADS
    nt = (((1,), (1,)), ((), ()))

    @pl.when(pl.program_id(1) == 0)
    def _():
        st_ref[...] = jnp.zeros_like(st_ref)

    rows = lax.broadcasted_iota(jnp.int32, (SUB, SUB), 0)
    cols = lax.broadcasted_iota(jnp.int32, (SUB, SUB), 1)
    same_chunk = (rows // C) == (cols // C)
    causal = same_chunk & (rows >= cols)
    low = jnp.where(causal, 1.0, 0.0).astype(BF16)
    blk = jnp.where(same_chunk, 1.0, 0.0).astype(BF16)
    lane = lax.broadcasted_iota(jnp.int32, (1, GLA_K_WIDTH), 1)
    in_head = [(lane >= h * GLA_DK) & (lane < (h + 1) * GLA_DK) for h in range(H)]
    gain = gain_ref[...]

    st = st_ref[...]
    for sb in range(GLA_TS // SUB):
        r0 = sb * SUB
        hi, lo = _split_hi_lo(lg_ref[r0:r0 + SUB, :])
        b = jnp.dot(low, hi, preferred_element_type=F32) + jnp.dot(low, lo, preferred_element_type=F32)
        b_last = jnp.dot(blk, hi, preferred_element_type=F32) + jnp.dot(blk, lo, preferred_element_type=F32)
        gq = gq_ref[r0:r0 + SUB, :].astype(F32)
        gk = gk_ref[r0:r0 + SUB, :].astype(F32)
        q_dec = gq * jnp.exp(b) * (GLA_DK ** -0.5)
        k_inv = (gk * jnp.exp(-b)).astype(BF16)
        k_dec = (gk * jnp.exp(b_last - b)).astype(BF16)
        decay = jnp.exp(b_last)
        qh = [jnp.where(in_head[h], q_dec, 0.0).astype(BF16) for h in range(H)]

        intra = []
        for h in range(H):
            s = lax.dot_general(qh[h], k_inv, nt, preferred_element_type=F32)
            s = jnp.where(causal, s, 0.0).astype(BF16)
            vh = gv_ref[r0:r0 + SUB, h * GLA_DV:(h + 1) * GLA_DV]
            intra.append(jnp.dot(s, vh, preferred_element_type=F32))

        inter = [[] for _ in range(H)]
        for c in range(SUB // C):
            c0 = c * C
            q_stack = jnp.concatenate([qh[h][c0:c0 + C] for h in range(H)], axis=0)
            res = lax.dot_general(q_stack, st.astype(BF16), nt, preferred_element_type=F32)
            for h in range(H):
                inter[h].append(res[h * C:(h + 1) * C, h * GLA_DV:(h + 1) * GLA_DV])
            v_t = gv_ref[r0 + c0:r0 + c0 + C, :].astype(F32).T.astype(BF16)
            kv = jnp.dot(v_t, k_dec[c0:c0 + C], preferred_element_type=F32)
            st = st * decay[c0:c0 + 1, :] + kv

        for h in range(H):
            o = intra[h] + jnp.concatenate(inter[h], axis=0)
            y = o * lax.rsqrt(jnp.mean(o * o, axis=-1, keepdims=True) + EPS) * gain
            g = gr_ref[r0:r0 + SUB, h * GLA_DV:(h + 1) * GLA_DV].astype(F32)
            y = y * (g * (1.0 / (1.0 + jnp.exp(-g))))
            o_ref[r0:r0 + SUB, h * GLA_DV:(h + 1) * GLA_DV] = y.astype(BF16)
    st_ref[...] = st


def _gla_call(gq, gk, gv, gr, lg, gain, batch, seq):
    N = gq.shape[0]
    nt = seq // GLA_TS
    row = lambda b, t: (b * nt + t, 0)
    return pl.pallas_call(
        _gla_kernel,
        grid=(batch, nt),
        in_specs=[
            pl.BlockSpec((GLA_TS, GLA_K_WIDTH), row),
            pl.BlockSpec((GLA_TS, GLA_K_WIDTH), row),
            pl.BlockSpec((GLA_TS, GLA_V_WIDTH), row),
            pl.BlockSpec((GLA_TS, GLA_V_WIDTH), row),
            pl.BlockSpec((GLA_TS, GLA_K_WIDTH), row),
            pl.BlockSpec((1, GLA_DV), lambda b, t: (0, 0)),
        ],
        out_specs=pl.BlockSpec((GLA_TS, GLA_V_WIDTH), row),
        out_shape=jax.ShapeDtypeStruct((N, GLA_V_WIDTH), BF16),
        scratch_shapes=[pltpu.VMEM((GLA_V_WIDTH, GLA_K_WIDTH), F32)],
        compiler_params=_params(("parallel", "arbitrary")),
        name="gla",
    )(gq, gk, gv, gr, lg, gain)


def _out_kernel(a_ref, b_ref, x_ref, mod_ref, g_ref, wa_ref, wb_ref, wrh_ref, wrl_ref, br_ref,
                x1_ref, h2a_ref, h2b_ref, route_ref, route_t_ref, cnt_ref, run_ref):
    @pl.when(pl.program_id(0) == 0)
    def _():
        run_ref[...] = jnp.zeros_like(run_ref)

    gate_m = mod_ref[2:3, :]
    shift_f = mod_ref[3:4, :]
    scale_f = mod_ref[4:5, :]
    y = jnp.dot(a_ref[...], wa_ref[...], preferred_element_type=F32)
    y = y + jnp.dot(b_ref[...], wb_ref[...], preferred_element_type=F32)
    x1 = x_ref[...] + gate_m * y
    x1_ref[...] = x1
    n = x1 * lax.rsqrt(jnp.mean(x1 * x1, axis=-1, keepdims=True) + EPS) * g_ref[...]
    h2 = n * (1.0 + scale_f) + shift_f
    h_hi, h_lo = _split_hi_lo(h2)
    h2a_ref[...], h2b_ref[...] = _pack_rows(h2)

    wr_hi = wrh_ref[...]
    logits = jnp.dot(h_hi, wr_hi, preferred_element_type=F32)
    logits = logits + jnp.dot(h_lo, wr_hi, preferred_element_type=F32)
    logits = logits + jnp.dot(h_hi, wrl_ref[...], preferred_element_type=F32)
    logits = logits + br_ref[...]

    lane = lax.broadcasted_iota(jnp.int32, logits.shape, 1)
    neg = jnp.float32(-jnp.inf)
    big = jnp.int32(1 << 20)
    is_grp = (lane >= N_EXPERTS) & (lane < N_EXPERTS + N_GROUPS)
    glog = jnp.where(is_grp, logits, neg)
    gmax = jnp.max(glog, axis=-1, keepdims=True)
    gidx = jnp.min(jnp.where(glog == gmax, lane - N_EXPERTS, big), axis=-1, keepdims=True)
    grp_w = 1.0 / jnp.sum(jnp.where(is_grp, jnp.exp(glog - gmax), 0.0), axis=-1, keepdims=True)

    lo_lane = gidx * EXPERTS_PER_GROUP
    in_grp = (lane >= lo_lane) & (lane < lo_lane + EXPERTS_PER_GROUP)
    elog = jnp.where(in_grp, logits, neg)
    m1 = jnp.max(elog, axis=-1, keepdims=True)
    i1 = jnp.min(jnp.where(elog == m1, lane, big), axis=-1, keepdims=True)
    elog2 = jnp.where(lane == i1, neg, elog)
    m2 = jnp.max(elog2, axis=-1, keepdims=True)
    i2 = jnp.min(jnp.where(elog2 == m2, lane, big), axis=-1, keepdims=True)
    e21 = jnp.exp(m2 - m1)
    p1 = 1.0 / (1.0 + e21)
    p2 = e21 * p1

    tm = logits.shape[0]
    sel1 = lane == i1
    sel2 = lane == i2
    onehot = jnp.where(sel1 | sel2, 1.0, 0.0)
    rr = lax.broadcasted_iota(jnp.int32, (tm, tm), 0)
    cc = lax.broadcasted_iota(jnp.int32, (tm, tm), 1)
    before = jnp.where(rr > cc, 1.0, 0.0).astype(BF16)
    seen = jnp.dot(before, onehot.astype(BF16), preferred_element_type=F32) + run_ref[0:1, :]
    rank1 = jnp.sum(jnp.where(sel1, seen, 0.0), axis=-1, keepdims=True)
    rank2 = jnp.sum(jnp.where(sel2, seen, 0.0), axis=-1, keepdims=True)
    run = run_ref[...] + jnp.sum(onehot, axis=0, keepdims=True)
    run_ref[...] = run
    cnt_ref[...] = run

    fields = (i1.astype(F32), i2.astype(F32), p1 * grp_w, p2 * grp_w, rank1, rank2)
    route = jnp.zeros_like(logits)
    for idx, val in enumerate(fields):
        route = jnp.where(lane == idx, val, route)
    route_ref[...] = route
    route_t_ref[...] = route.T[:SUBLANES, :]


def _out_call(oa, ob, x, mod, g, wa, wb, wr_hi, wr_lo, br, seq):
    N, D = x.shape
    tm = TM_OUT
    per_b = seq // tm
    row = lambda i: (i, 0)
    const = lambda i: (0, 0)
    return pl.pallas_call(
        _out_kernel,
        grid=(N // tm,),
        in_specs=[
            pl.BlockSpec((tm, SB_WIDTH), row),
            pl.BlockSpec((tm, GLA_V_WIDTH), row),
            pl.BlockSpec((tm, D), row),
            pl.BlockSpec((None, N_MOD, D), lambda i: (i // per_b, 0, 0)),
            pl.BlockSpec((1, D), const),
            pl.BlockSpec((SB_WIDTH, D), const),
            pl.BlockSpec((GLA_V_WIDTH, D), const),
            pl.BlockSpec((D, LANES), const),
            pl.BlockSpec((D, LANES), const),
            pl.BlockSpec((1, LANES), const),
        ],
        out_specs=[pl.BlockSpec((tm, D), row), pl.BlockSpec((tm, D_PACK), row), pl.BlockSpec((tm, D_PACK), row),
                   pl.BlockSpec((tm, LANES), row), pl.BlockSpec((SUBLANES, tm), lambda i: (0, i)),
                   pl.BlockSpec((SUBLANES, LANES), const)],
        out_shape=[jax.ShapeDtypeStruct((N, D), F32), jax.ShapeDtypeStruct((N, D_PACK), U32),
                   jax.ShapeDtypeStruct((N, D_PACK), U32), jax.ShapeDtypeStruct((N, LANES), F32),
                   jax.ShapeDtypeStruct((SUBLANES, N), F32), jax.ShapeDtypeStruct((SUBLANES, LANES), F32)],
        scratch_shapes=[pltpu.VMEM((SUBLANES, LANES), F32)],
        compiler_params=_params(("arbitrary",)),
        name="out_proj_router",
    )(oa, ob, x, mod, g, wa, wb, wr_hi, wr_lo, br)


def _sc_mesh():
    return plsc.VectorSubcoreMesh(core_axis_name="c", subcore_axis_name="s")


def _sc_dispatch(halves, pos, n_rows):
    N, dh = halves[0].shape
    w = SC_WINDOW
    nblk = N // w
    out_type = [jax.ShapeDtypeStruct((n_rows, dh), h.dtype) for h in halves]

    @pl.kernel(out_type=out_type, mesh=_sc_mesh(), scratch_types=[], name="moe_dispatch")
    def run(xa_hbm, xb_hbm, i_hbm, oa_hbm, ob_hbm):
        for x_hbm, o_hbm in ((xa_hbm, oa_hbm), (xb_hbm, ob_hbm)):
            def body(x_vmem, i_vmem, o_hbm=o_hbm):
                pltpu.sync_copy(x_vmem, o_hbm.at[i_vmem.at[0]])

            pltpu.emit_pipeline(
                body,
                grid=(2 * nblk,),
                in_specs=[pl.BlockSpec((w, dh), lambda j: (j % nblk, 0)),
                          pl.BlockSpec((1, w), lambda j: (0, j))],
                out_specs=[],
                core_axis_name=("c", "s"),
                dimension_semantics=(pltpu.PARALLEL,),
            )(x_hbm, i_hbm)

    return run(halves[0], halves[1], pos)


def _sc_collect(halves, pos):
    dh = halves[0].shape[1]
    n_idx = pos.shape[1]
    w = SC_WINDOW
    out_type = [jax.ShapeDtypeStruct((n_idx, dh), h.dtype) for h in halves]

    @pl.kernel(out_type=out_type, mesh=_sc_mesh(), scratch_types=[], name="moe_collect")
    def run(ya_hbm, yb_hbm, i_hbm, oa_hbm, ob_hbm):
        for y_hbm, o_hbm in ((ya_hbm, oa_hbm), (yb_hbm, ob_hbm)):
            def body(i_vmem, o_vmem, y_hbm=y_hbm):
                pltpu.sync_copy(y_hbm.at[i_vmem.at[0]], o_vmem)

            pltpu.emit_pipeline(
                body,
                grid=(n_idx // w,),
                in_specs=[pl.BlockSpec((1, w), lambda j: (0, j))],
                out_specs=[pl.BlockSpec((w, dh), lambda j: (j, 0))],
                core_axis_name=("c", "s"),
                dimension_semantics=(pltpu.PARALLEL,),
            )(i_hbm, o_hbm)

    return run(halves[0], halves[1], pos)


def _experts_kernel(te_ref, na_ref, xa_ref, xb_ref, wg_ref, wu_ref, wd_ref, oa_ref, ob_ref):
    del te_ref

    @pl.when(pl.program_id(0) < na_ref[0])
    def _():
        x = _unpack_rows(xa_ref[...], xb_ref[...]).astype(BF16)
        a = jnp.dot(x, wg_ref[...].astype(BF16), preferred_element_type=F32)
        u = jnp.dot(x, wu_ref[...].astype(BF16), preferred_element_type=F32)
        act = (a * (1.0 / (1.0 + jnp.exp(-a)))) * u
        y = jnp.dot(act.astype(BF16), wd_ref[...].astype(BF16), preferred_element_type=F32)
        oa_ref[...], ob_ref[...] = _pack_rows(y)


def _experts_call(tile_expert, n_active, xs, wg, wu, wd):
    R, dh = xs[0].shape
    tm = TM_MOE
    rows = lambda i, te, na: (jnp.minimum(i, na[0] - 1), 0)
    grid_spec = pltpu.PrefetchScalarGridSpec(
        num_scalar_prefetch=2,
        grid=(R // tm,),
        in_specs=[
            pl.BlockSpec((tm, dh), rows),
            pl.BlockSpec((tm, dh), rows),
            pl.BlockSpec((None, D_MODEL, D_EXPERT), lambda i, te, na: (te[i], 0, 0)),
            pl.BlockSpec((None, D_MODEL, D_EXPERT), lambda i, te, na: (te[i], 0, 0)),
            pl.BlockSpec((None, D_EXPERT, D_MODEL), lambda i, te, na: (te[i], 0, 0)),
        ],
        out_specs=[pl.BlockSpec((tm, dh), rows), pl.BlockSpec((tm, dh), rows)],
    )
    return pl.pallas_call(
        _experts_kernel,
        grid_spec=grid_spec,
        out_shape=[jax.ShapeDtypeStruct((R, dh), U32)] * 2,
        compiler_params=_params(("arbitrary",)),
        name="moe_experts",
    )(tile_expert, n_active, xs[0], xs[1], wg, wu, wd)


def _combine_kernel(x1_ref, y1a_ref, y1b_ref, y2a_ref, y2b_ref, route_ref, mod_ref, o_ref):
    route = route_ref[...]
    lane = lax.broadcasted_iota(jnp.int32, route.shape, 1)
    w1 = jnp.sum(jnp.where(lane == 2, route, 0.0), axis=-1, keepdims=True)
    w2 = jnp.sum(jnp.where(lane == 3, route, 0.0), axis=-1, keepdims=True)
    y1 = _unpack_rows(y1a_ref[...], y1b_ref[...])
    y2 = _unpack_rows(y2a_ref[...], y2b_ref[...])
    o_ref[...] = x1_ref[...] + mod_ref[5:6, :] * (w1 * y1 + w2 * y2)


def _combine_call(x1, ys, route, mod, seq):
    N, D = x1.shape
    tm = TM_OUT
    per_b = seq // tm
    nblk = N // tm
    dh = ys[0].shape[1]
    row = lambda i: (i, 0)
    second = lambda i: (i + nblk, 0)
    return pl.pallas_call(
        _combine_kernel,
        grid=(nblk,),
        in_specs=[
            pl.BlockSpec((tm, D), row),
            pl.BlockSpec((tm, dh), row),
            pl.BlockSpec((tm, dh), row),
            pl.BlockSpec((tm, dh), second),
            pl.BlockSpec((tm, dh), second),
            pl.BlockSpec((tm, LANES), row),
            pl.BlockSpec((None, N_MOD, D), lambda i: (i // per_b, 0, 0)),
        ],
        out_specs=pl.BlockSpec((tm, D), row),
        out_shape=jax.ShapeDtypeStruct((N, D), F32),
        compiler_params=_params(("parallel",)),
        name="moe_combine",
    )(x1, ys[0], ys[1], ys[0], ys[1], route, mod)


def _routing_tables(route_t, counts, n_rows):
    tm = TM_MOE
    cnt = counts[0, :N_EXPERTS].astype(jnp.int32)
    padded = ((cnt + tm - 1) // tm) * tm
    ends = jnp.cumsum(padded)
    base = ends - padded
    expert_ids = jnp.arange(N_EXPERTS, dtype=jnp.int32)[:, None]

    def rows_of(expert, rank):
        first = jnp.sum(jnp.where(expert_ids == expert[None, :], base[:, None], 0), axis=0)
        return first + rank

    rt = route_t.astype(jnp.int32)
    pos = jnp.concatenate([rows_of(rt[0], rt[4]), rows_of(rt[1], rt[5])]).reshape(1, -1)
    n_active = (ends[-1] // tm).reshape(1)
    tile_start = jnp.arange(n_rows // tm, dtype=jnp.int32) * tm
    tile_start = jnp.minimum(tile_start, ends[-1] - tm)
    tile_expert = jnp.sum(tile_start[:, None] >= ends[None, :], axis=1).astype(jnp.int32)
    return pos, tile_expert, n_active


def kernel(x, c, w_ada, b_ada, norm_mix, norm_ffn, w_in, w_gk2, b_gk, q_gain, k_gain, gla_gain, w_out,
           w_router_grp, b_router_grp, w_router_exp, b_router_exp, w_gate, w_up, w_down):
    B, S, D = x.shape
    L = w_ada.shape[0]
    N = B * S

    mod_all = _ada_call(c, w_ada, b_ada).reshape(L, B, N_MOD, D)

    w_in_p = jnp.pad(w_in, ((0, 0), (0, 0), (0, D_IN_PAD - w_in.shape[-1]))).astype(BF16)
    w_gk2_p = jnp.pad(w_gk2, ((0, 0), (0, LANES - GLA_RANK), (0, 0))).astype(BF16)
    qg = jnp.tile(q_gain, (1, SB_HEADS)).reshape(L, 1, SB_WIDTH)
    kg = jnp.tile(k_gain, (1, SB_HEADS)).reshape(L, 1, SB_WIDTH)
    w_out_b = w_out.astype(BF16)
    w_r = jnp.concatenate([w_router_exp, w_router_grp], axis=-1)
    w_r = jnp.pad(w_r, ((0, 0), (0, 0), (0, LANES - w_r.shape[-1])))
    w_r_hi = w_r.astype(BF16)
    w_r_lo = (w_r - w_r_hi.astype(F32)).astype(BF16)
    b_r = jnp.concatenate([b_router_exp, b_router_grp], axis=-1)
    b_r = jnp.pad(b_r, ((0, 0), (0, LANES - b_r.shape[-1]))).reshape(L, 1, LANES)
    wg = w_gate.reshape(L, N_EXPERTS, D, D_EXPERT)
    wu = w_up.reshape(L, N_EXPERTS, D, D_EXPERT)
    wd = w_down.reshape(L, N_EXPERTS, D_EXPERT, D)

    n_rows = 2 * N + N_EXPERTS * TM_MOE
    xf = x.reshape(N, D)
    for l in range(L):
        mod = mod_all[l]
        q, k, v, gq, gk, gv, gr, lg = _proj_call(
            xf, mod, norm_mix[l].reshape(1, D), w_in_p[l], qg[l], kg[l], w_gk2_p[l],
            b_gk[l].reshape(1, GLA_K_WIDTH), S)
        out_a = _sb_call(q, k, v, B, S)
        out_b = _gla_call(gq, gk, gv, gr, lg, gla_gain[l].reshape(1, GLA_DV), B, S)
        x1, h2a, h2b, route, route_t, counts = _out_call(out_a, out_b, xf, mod, norm_ffn[l].reshape(1, D),
                                          w_out_b[l, :SB_WIDTH], w_out_b[l, SB_WIDTH:], w_r_hi[l], w_r_lo[l],
                                          b_r[l], S)
        pos, tile_expert, n_active = _routing_tables(route_t, counts, n_rows)
        xs = _sc_dispatch((h2a, h2b), pos, n_rows)
        ys = _experts_call(tile_expert, n_active, xs, wg[l], wu[l], wd[l])
        yt = _sc_collect(ys, pos)
        xf = _combine_call(x1, yt, route, mod, S)
    return xf.reshape(B, S, D)
```

```python
import functools

import jax
import jax.numpy as jnp
from jax import lax
from jax.experimental import pallas as pl
from jax.experimental.pallas import tpu as pltpu
from jax.experimental.pallas import tpu_sc as plsc

F32 = jnp.float32
BF16 = jnp.bfloat16

D_MODEL = 1024
SB_HEADS = 8
SB_HEAD_DIM = 64
SB_WIDTH = 512
GLA_HEADS = 4
GLA_DK = 64
GLA_DV = 128
GLA_K_WIDTH = 256
GLA_V_WIDTH = 512
GLA_RANK = 16
GLA_GATE_NORM = 16.0
GLA_CHUNK = 64
N_GROUPS = 4
EXPERTS_PER_GROUP = 8
N_EXPERTS = N_GROUPS * EXPERTS_PER_GROUP
D_EXPERT = 256
N_MOD = 6
EPS = 1e-6

LANES = 128
SUBLANES = 8
VMEM_LIMIT = 56 * 1024 * 1024

_C_Q, _C_K, _C_V = 0, 512, 1024
_C_GQ, _C_GK, _C_GV, _C_GR, _C_LR = 1536, 1792, 2048, 2560, 3072
D_IN_PAD = 3200

TM_PROJ = 512
TM_OUT = 512
TM_MOE = 512
SC_WINDOW = 128
D_PACK = D_MODEL // 4
U32 = jnp.uint32
SB_BQ = 256
SB_PAIRS = 2
GLA_TS = 512
GLA_SUB = 256
LOG2E = 1.4426950408889634
SB_SKIP_THRESH = 160.0


def _params(sem):
    return pltpu.CompilerParams(dimension_semantics=sem, vmem_limit_bytes=VMEM_LIMIT)


def _split_hi_lo(x):
    hi = x.astype(BF16)
    lo = (x - hi.astype(F32)).astype(BF16)
    return hi, lo


def _pack_rows(x):
    r = pltpu.bitcast(x.astype(BF16).astype(F32), U32)
    out = []
    for half in range(2):
        c0 = half * 2 * D_PACK
        out.append(r[:, c0:c0 + D_PACK] | (r[:, c0 + D_PACK:c0 + 2 * D_PACK] >> 16))
    return out


def _unpack_rows(pa, pb):
    parts = []
    for p in (pa, pb):
        parts.append(pltpu.bitcast(p & U32(0xFFFF0000), F32))
        parts.append(pltpu.bitcast(p << 16, F32))
    return jnp.concatenate(parts, axis=1)


def _softplus2(z2):
    neg_abs = pltpu.bitcast(pltpu.bitcast(z2, jnp.uint32) | jnp.uint32(0x80000000), F32)
    return jnp.maximum(z2, 0.0) + jnp.log2(1.0 + jnp.exp2(neg_abs))


def _ada_kernel(c_ref, w_ref, b_ref, o_ref):
    c = c_ref[...]
    ca = (c * (1.0 / (1.0 + jnp.exp(-c)))).astype(BF16)
    o_ref[...] = jnp.dot(ca, w_ref[...].astype(BF16), preferred_element_type=F32) + b_ref[...]


def _ada_call(c, w_ada, b_ada):
    L, D, _ = w_ada.shape
    B = c.shape[0]
    return pl.pallas_call(
        _ada_kernel,
        grid=(L, N_MOD),
        in_specs=[
            pl.BlockSpec((B, D), lambda l, j: (0, 0)),
            pl.BlockSpec((None, D, D), lambda l, j: (l, 0, j)),
            pl.BlockSpec((None, 1, D), lambda l, j: (l, 0, j)),
        ],
        out_specs=pl.BlockSpec((None, B, D), lambda l, j: (l, 0, j)),
        out_shape=jax.ShapeDtypeStruct((L, B, N_MOD * D), F32),
        compiler_params=_params(("parallel", "parallel")),
        name="ada_mod",
    )(c, w_ada, b_ada.reshape(L, 1, N_MOD * D))


def _proj_kernel(x_ref, mod_ref, g_ref, w_ref, qg_ref, kg_ref, wgk_ref, bgk_ref,
                 q_ref, k_ref, v_ref, gq_ref, gk_ref, gv_ref, gr_ref, lg_ref):
    x = x_ref[...]
    shift = mod_ref[0:1, :]
    scale = mod_ref[1:2, :]
    y = x * lax.rsqrt(jnp.mean(x * x, axis=-1, keepdims=True) + EPS) * g_ref[...]
    h = (y * (1.0 + scale) + shift).astype(BF16)

    def proj(c0, width):
        return jnp.dot(h, w_ref[:, c0:c0 + width], preferred_element_type=F32)

    r = lax.broadcasted_iota(jnp.int32, (SB_WIDTH, SB_WIDTH), 0) // SB_HEAD_DIM
    c = lax.broadcasted_iota(jnp.int32, (SB_WIDTH, SB_WIDTH), 1) // SB_HEAD_DIM
    avg = jnp.where(r == c, 1.0 / SB_HEAD_DIM, 0.0).astype(BF16)

    def head_norm(t, gain):
        ms = jnp.dot((t * t).astype(BF16), avg, preferred_element_type=F32)
        return t * lax.rsqrt(ms + EPS) * gain

    q = head_norm(proj(_C_Q, SB_WIDTH), qg_ref[...])
    q_ref[...] = (q * (SB_HEAD_DIM ** -0.5 * LOG2E)).astype(BF16)
    k_ref[...] = head_norm(proj(_C_K, SB_WIDTH), kg_ref[...]).astype(BF16)
    v_ref[...] = proj(_C_V, SB_WIDTH).astype(BF16)
    gq_ref[...] = proj(_C_GQ, GLA_K_WIDTH).astype(BF16)
    gk_ref[...] = proj(_C_GK, GLA_K_WIDTH).astype(BF16)
    gv_ref[...] = proj(_C_GV, GLA_V_WIDTH).astype(BF16)
    gr_ref[...] = proj(_C_GR, GLA_V_WIDTH).astype(BF16)
    lr = proj(_C_LR, LANES).astype(BF16)
    pre = jnp.dot(lr, wgk_ref[...], preferred_element_type=F32) + bgk_ref[...]
    lg_ref[...] = (jnp.minimum(pre, 0.0) - jnp.log(1.0 + jnp.exp(-jnp.abs(pre)))) * (1.0 / GLA_GATE_NORM)


def _proj_call(x, mod, g, w_in, q_gain, k_gain, w_gk2, b_gk, seq, layer):
    N, D = x.shape
    tm = TM_PROJ
    per_b = seq // tm
    row = lambda i: (i, 0)
    const = lambda i: (0, 0)
    widths = [(SB_WIDTH, BF16)] * 3 + [(GLA_K_WIDTH, BF16)] * 2 + [(GLA_V_WIDTH, BF16)] * 2 + [(GLA_K_WIDTH, F32)]
    return pl.pallas_call(
        _proj_kernel,
        grid=(N // tm,),
        in_specs=[
            pl.BlockSpec((tm, D), row),
            pl.BlockSpec((None, N_MOD, D), lambda i: (i // per_b, 0, 0)),
            pl.BlockSpec((1, D), const),
            pl.BlockSpec((None, D, D_IN_PAD), lambda i: (layer, 0, 0)),
            pl.BlockSpec((1, SB_WIDTH), const),
            pl.BlockSpec((1, SB_WIDTH), const),
            pl.BlockSpec((LANES, GLA_K_WIDTH), const),
            pl.BlockSpec((1, GLA_K_WIDTH), const),
        ],
        out_specs=[pl.BlockSpec((tm, w), row) for w, _ in widths],
        out_shape=[jax.ShapeDtypeStruct((N, w), dt) for w, dt in widths],
        compiler_params=_params(("parallel",)),
        name="in_proj",
    )(x, mod, g, w_in, q_gain, k_gain, w_gk2, b_gk)


def _sb_kernel(q_ref, k_ref, v_ref, o_ref):
    i = pl.program_id(2)
    bq = SB_BQ
    lane = lax.broadcasted_iota(jnp.int32, (1, LANES), 1)
    qms = []
    for pr in range(SB_PAIRS):
        q = q_ref[:, pr * LANES:(pr + 1) * LANES]
        for hh in range(2):
            in_head = (lane >= hh * SB_HEAD_DIM) & (lane < (hh + 1) * SB_HEAD_DIM)
            qms.append(jnp.where(in_head, q, jnp.zeros_like(q)))
    nh = len(qms)

    t_loc = lax.broadcasted_iota(jnp.int32, (bq, bq), 0)
    s_loc = lax.broadcasted_iota(jnp.int32, (bq, bq), 1)
    diag_mask = s_loc < t_loc
    later = jnp.where(t_loc > s_loc, 1.0, 0.0).astype(BF16)

    def step(k_start, cs, accs, mask):
        new_cs, new_accs = [], []
        for hh in range(nh):
            pr = hh // 2
            kb = k_ref[pl.ds(k_start, bq), pr * LANES:(pr + 1) * LANES]
            vb = v_ref[pl.ds(k_start, bq), pr * LANES:(pr + 1) * LANES]
            z = lax.dot_general(qms[hh], kb, (((1,), (1,)), ((), ())), preferred_element_type=F32)
            sp = _softplus2(z)
            lf = sp if mask is None else jnp.where(mask, sp, 0.0)
            lf = lf.astype(BF16)
            after = jnp.dot(lf, later, preferred_element_type=F32)
            w = jnp.exp2(z - sp - after - cs[hh])
            if mask is not None:
                w = jnp.where(mask, w, 0.0)
            new_accs.append(accs[hh] + jnp.dot(w.astype(BF16), vb, preferred_element_type=F32))
            new_cs.append(cs[hh] + after[:, 0:1] + lf[:, 0:1].astype(F32))
        return new_cs, new_accs

    c_zero = [jnp.zeros((bq, 1), F32)] * nh
    acc_zero = [jnp.zeros((bq, LANES), F32)] * nh

    def diag_only():
        cs, accs = step(pl.multiple_of(i * bq, bq), c_zero, acc_zero, diag_mask)
        return tuple(cs) + tuple(accs)

    def diag_and_previous():
        cs, accs = step(pl.multiple_of(i * bq, bq), c_zero, acc_zero, diag_mask)
        cs, accs = step(pl.multiple_of((i - 1) * bq, bq), cs, accs, None)
        return tuple(cs) + tuple(accs)

    state0 = lax.cond(i == 0, diag_only, diag_and_previous)

    def cond(state):
        c_min = functools.reduce(jnp.minimum, state[1:1 + nh])
        return jnp.logical_and(state[0] >= 0, jnp.min(c_min) < SB_SKIP_THRESH)

    def body(state):
        j = state[0]
        cs, accs = step(pl.multiple_of(j * bq, bq), list(state[1:1 + nh]), list(state[1 + nh:]), None)
        return (j - 1,) + tuple(cs) + tuple(accs)

    final = lax.while_loop(cond, body, (i - 2,) + tuple(state0))
    accs = final[1 + nh:]
    for pr in range(SB_PAIRS):
        o_ref[:, pr * LANES:(pr + 1) * LANES] = jnp.where(
            lane < SB_HEAD_DIM, accs[2 * pr], accs[2 * pr + 1]).astype(BF16)


def _sb_call(q, k, v, batch, seq):
    N = q.shape[0]
    nq = seq // SB_BQ
    width = SB_PAIRS * LANES
    return pl.pallas_call(
        _sb_kernel,
        grid=(batch, SB_WIDTH // width, nq),
        in_specs=[
            pl.BlockSpec((SB_BQ, width), lambda b, p, i: (b * nq + i, p)),
            pl.BlockSpec((seq, width), lambda b, p, i: (b, p)),
            pl.BlockSpec((seq, width), lambda b, p, i: (b, p)),
        ],
        out_specs=pl.BlockSpec((SB_BQ, width), lambda b, p, i: (b * nq + i, p)),
        out_shape=jax.ShapeDtypeStruct((N, SB_WIDTH), BF16),
        compiler_params=_params(("parallel", "parallel", "arbitrary")),
        name="sb_attn",
    )(q, k, v)


def _gla_kernel(gq_ref, gk_ref, gv_ref, gr_ref, lg_ref, gain_ref, o_ref, st_ref):
    C, SUB, H = GLA_CHUNK, GLA_SUB, GLA_HEADS
    nt = (((1,), (1,)), ((), ()))

    @pl.when(pl.program_id(1) == 0)
    def _():
        st_ref[...] = jnp.zeros_like(st_ref)

    rows = lax.broadcasted_iota(jnp.int32, (SUB, SUB), 0)
    cols = lax.broadcasted_iota(jnp.int32, (SUB, SUB), 1)
    same_chunk = (rows // C) == (cols // C)
    causal = same_chunk & (rows >= cols)
    low = jnp.where(causal, 1.0, 0.0).astype(BF16)
    blk = jnp.where(same_chunk, 1.0, 0.0).astype(BF16)
    lane = lax.broadcasted_iota(jnp.int32, (1, GLA_K_WIDTH), 1)
    in_head = [(lane >= h * GLA_DK) & (lane < (h + 1) * GLA_DK) for h in range(H)]
    gain = gain_ref[...]

    st = st_ref[...]
    for sb in range(GLA_TS // SUB):
        r0 = sb * SUB
        hi, lo = _split_hi_lo(lg_ref[r0:r0 + SUB, :])
        b = jnp.dot(low, hi, preferred_element_type=F32) + jnp.dot(low, lo, preferred_element_type=F32)
        b_last = jnp.dot(blk, hi, preferred_element_type=F32) + jnp.dot(blk, lo, preferred_element_type=F32)
        gq = gq_ref[r0:r0 + SUB, :].astype(F32)
        gk = gk_ref[r0:r0 + SUB, :].astype(F32)
        q_dec = gq * jnp.exp(b) * (GLA_DK ** -0.5)
        k_inv = (gk * jnp.exp(-b)).astype(BF16)
        k_dec = (gk * jnp.exp(b_last - b)).astype(BF16)
        decay = jnp.exp(b_last)
        qh = [jnp.where(in_head[h], q_dec, 0.0).astype(BF16) for h in range(H)]

        intra = []
        for h in range(H):
            s = lax.dot_general(qh[h], k_inv, nt, preferred_element_type=F32)
            s = jnp.where(causal, s, 0.0).astype(BF16)
            vh = gv_ref[r0:r0 + SUB, h * GLA_DV:(h + 1) * GLA_DV]
            intra.append(jnp.dot(s, vh, preferred_element_type=F32))

        inter = [[] for _ in range(H)]
        for c in range(SUB // C):
            c0 = c * C
            q_stack = jnp.concatenate([qh[h][c0:c0 + C] for h in range(H)], axis=0)
            res = lax.dot_general(q_stack, st.astype(BF16), nt, preferred_element_type=F32)
            for h in range(H):
                inter[h].append(res[h * C:(h + 1) * C, h * GLA_DV:(h + 1) * GLA_DV])
            v_t = gv_ref[r0 + c0:r0 + c0 + C, :].astype(F32).T.astype(BF16)
            kv = jnp.dot(v_t, k_dec[c0:c0 + C], preferred_element_type=F32)
            st = st * decay[c0:c0 + 1, :] + kv

        for h in range(H):
            o = intra[h] + jnp.concatenate(inter[h], axis=0)
            y = o * lax.rsqrt(jnp.mean(o * o, axis=-1, keepdims=True) + EPS) * gain
            g = gr_ref[r0:r0 + SUB, h * GLA_DV:(h + 1) * GLA_DV].astype(F32)
            y = y * (g * (1.0 / (1.0 + jnp.exp(-g))))
            o_ref[r0:r0 + SUB, h * GLA_DV:(h + 1) * GLA_DV] = y.astype(BF16)
    st_ref[...] = st


def _gla_call(gq, gk, gv, gr, lg, gain, batch, seq):
    N = gq.shape[0]
    nt = seq // GLA_TS
    row = lambda b, t: (b * nt + t, 0)
    return pl.pallas_call(
        _gla_kernel,
        grid=(batch, nt),
        in_specs=[
            pl.BlockSpec((GLA_TS, GLA_K_WIDTH), row),
            pl.BlockSpec((GLA_TS, GLA_K_WIDTH), row),
            pl.BlockSpec((GLA_TS, GLA_V_WIDTH), row),
            pl.BlockSpec((GLA_TS, GLA_V_WIDTH), row),
            pl.BlockSpec((GLA_TS, GLA_K_WIDTH), row),
            pl.BlockSpec((1, GLA_DV), lambda b, t: (0, 0)),
        ],
        out_specs=pl.BlockSpec((GLA_TS, GLA_V_WIDTH), row),
        out_shape=jax.ShapeDtypeStruct((N, GLA_V_WIDTH), BF16),
        scratch_shapes=[pltpu.VMEM((GLA_V_WIDTH, GLA_K_WIDTH), F32)],
        compiler_params=_params(("parallel", "arbitrary")),
        name="gla",
    )(gq, gk, gv, gr, lg, gain)


def _out_kernel(a_ref, b_ref, x_ref, mod_ref, g_ref, wa_ref, wb_ref, wrh_ref, wrl_ref, br_ref,
                x1_ref, h2a_ref, h2b_ref, route_ref, route_t_ref, cnt_ref, run_ref):
    @pl.when(pl.program_id(0) == 0)
    def _():
        run_ref[...] = jnp.zeros_like(run_ref)

    gate_m = mod_ref[2:3, :]
    shift_f = mod_ref[3:4, :]
    scale_f = mod_ref[4:5, :]
    y = jnp.dot(a_ref[...], wa_ref[...], preferred_element_type=F32)
    y = y + jnp.dot(b_ref[...], wb_ref[...], preferred_element_type=F32)
    x1 = x_ref[...] + gate_m * y
    x1_ref[...] = x1
    n = x1 * lax.rsqrt(jnp.mean(x1 * x1, axis=-1, keepdims=True) + EPS) * g_ref[...]
    h2 = n * (1.0 + scale_f) + shift_f
    h_hi, h_lo = _split_hi_lo(h2)
    h2a_ref[...], h2b_ref[...] = _pack_rows(h2)

    wr_hi = wrh_ref[...]
    logits = jnp.dot(h_hi, wr_hi, preferred_element_type=F32)
    logits = logits + jnp.dot(h_lo, wr_hi, preferred_element_type=F32)
    logits = logits + jnp.dot(h_hi, wrl_ref[...], preferred_element_type=F32)
    logits = logits + br_ref[...]

    lane = lax.broadcasted_iota(jnp.int32, logits.shape, 1)
    neg = jnp.float32(-jnp.inf)
    big = jnp.int32(1 << 20)
    is_grp = (lane >= N_EXPERTS) & (lane < N_EXPERTS + N_GROUPS)
    glog = jnp.where(is_grp, logits, neg)
    gmax = jnp.max(glog, axis=-1, keepdims=True)
    gidx = jnp.min(jnp.where(glog == gmax, lane - N_EXPERTS, big), axis=-1, keepdims=True)
    grp_w = 1.0 / jnp.sum(jnp.where(is_grp, jnp.exp(glog - gmax), 0.0), axis=-1, keepdims=True)

    lo_lane = gidx * EXPERTS_PER_GROUP
    in_grp = (lane >= lo_lane) & (lane < lo_lane + EXPERTS_PER_GROUP)
    elog = jnp.where(in_grp, logits, neg)
    m1 = jnp.max(elog, axis=-1, keepdims=True)
    i1 = jnp.min(jnp.where(elog == m1, lane, big), axis=-1, keepdims=True)
    elog2 = jnp.where(lane == i1, neg, elog)
    m2 = jnp.max(elog2, axis=-1, keepdims=True)
    i2 = jnp.min(jnp.where(elog2 == m2, lane, big), axis=-1, keepdims=True)
    e21 = jnp.exp(m2 - m1)
    p1 = 1.0 / (1.0 + e21)
    p2 = e21 * p1

    tm = logits.shape[0]
    sel1 = lane == i1
    sel2 = lane == i2
    onehot = jnp.where(sel1 | sel2, 1.0, 0.0)
    rr = lax.broadcasted_iota(jnp.int32, (tm, tm), 0)
    cc = lax.broadcasted_iota(jnp.int32, (tm, tm), 1)
    before = jnp.where(rr > cc, 1.0, 0.0).astype(BF16)
    seen = jnp.dot(before, onehot.astype(BF16), preferred_element_type=F32) + run_ref[0:1, :]
    rank1 = jnp.sum(jnp.where(sel1, seen, 0.0), axis=-1, keepdims=True)
    rank2 = jnp.sum(jnp.where(sel2, seen, 0.0), axis=-1, keepdims=True)
    run = run_ref[...] + jnp.sum(onehot, axis=0, keepdims=True)
    run_ref[...] = run
    cnt_ref[...] = run

    fields = (i1.astype(F32), i2.astype(F32), p1 * grp_w, p2 * grp_w, rank1, rank2)
    route = jnp.zeros_like(logits)
    for idx, val in enumerate(fields):
        route = jnp.where(lane == idx, val, route)
    route_ref[...] = route
    route_t_ref[...] = route.T[:SUBLANES, :]


def _out_call(oa, ob, x, mod, g, w_out, wr_hi, wr_lo, br, seq, layer):
    N, D = x.shape
    tm = TM_OUT
    per_b = seq // tm
    row = lambda i: (i, 0)
    const = lambda i: (0, 0)
    return pl.pallas_call(
        _out_kernel,
        grid=(N // tm,),
        in_specs=[
            pl.BlockSpec((tm, SB_WIDTH), row),
            pl.BlockSpec((tm, GLA_V_WIDTH), row),
            pl.BlockSpec((tm, D), row),
            pl.BlockSpec((None, N_MOD, D), lambda i: (i // per_b, 0, 0)),
            pl.BlockSpec((1, D), const),
            pl.BlockSpec((None, SB_WIDTH, D), lambda i: (layer, 0, 0)),
            pl.BlockSpec((None, GLA_V_WIDTH, D), lambda i: (layer, 1, 0)),
            pl.BlockSpec((None, D, LANES), lambda i: (layer, 0, 0)),
            pl.BlockSpec((None, D, LANES), lambda i: (layer, 0, 0)),
            pl.BlockSpec((1, LANES), const),
        ],
        out_specs=[pl.BlockSpec((tm, D), row), pl.BlockSpec((tm, D_PACK), row), pl.BlockSpec((tm, D_PACK), row),
                   pl.BlockSpec((tm, LANES), row), pl.BlockSpec((SUBLANES, tm), lambda i: (0, i)),
                   pl.BlockSpec((SUBLANES, LANES), const)],
        out_shape=[jax.ShapeDtypeStruct((N, D), F32), jax.ShapeDtypeStruct((N, D_PACK), U32),
                   jax.ShapeDtypeStruct((N, D_PACK), U32), jax.ShapeDtypeStruct((N, LANES), F32),
                   jax.ShapeDtypeStruct((SUBLANES, N), F32), jax.ShapeDtypeStruct((SUBLANES, LANES), F32)],
        scratch_shapes=[pltpu.VMEM((SUBLANES, LANES), F32)],
        compiler_params=_params(("arbitrary",)),
        name="out_proj_router",
    )(oa, ob, x, mod, g, w_out, w_out, wr_hi, wr_lo, br)


def _sc_mesh():
    return plsc.VectorSubcoreMesh(core_axis_name="c", subcore_axis_name="s")


def _sc_dispatch(halves, pos, n_rows):
    N, dh = halves[0].shape
    w = SC_WINDOW
    nblk = N // w
    out_type = [jax.ShapeDtypeStruct((n_rows, dh), h.dtype) for h in halves]

    @pl.kernel(out_type=out_type, mesh=_sc_mesh(), scratch_types=[], name="moe_dispatch")
    def run(xa_hbm, xb_hbm, i_hbm, oa_hbm, ob_hbm):
        for x_hbm, o_hbm in ((xa_hbm, oa_hbm), (xb_hbm, ob_hbm)):
            def body(x_vmem, i_vmem, o_hbm=o_hbm):
                pltpu.sync_copy(x_vmem, o_hbm.at[i_vmem.at[0]])

            pltpu.emit_pipeline(
                body,
                grid=(2 * nblk,),
                in_specs=[pl.BlockSpec((w, dh), lambda j: (j % nblk, 0)),
                          pl.BlockSpec((1, w), lambda j: (0, j))],
                out_specs=[],
                core_axis_name=("c", "s"),
                dimension_semantics=(pltpu.PARALLEL,),
            )(x_hbm, i_hbm)

    return run(halves[0], halves[1], pos)


def _sc_collect(halves, pos):
    dh = halves[0].shape[1]
    n_idx = pos.shape[1]
    w = SC_WINDOW
    out_type = [jax.ShapeDtypeStruct((n_idx, dh), h.dtype) for h in halves]

    @pl.kernel(out_type=out_type, mesh=_sc_mesh(), scratch_types=[], name="moe_collect")
    def run(ya_hbm, yb_hbm, i_hbm, oa_hbm, ob_hbm):
        for y_hbm, o_hbm in ((ya_hbm, oa_hbm), (yb_hbm, ob_hbm)):
            def body(i_vmem, o_vmem, y_hbm=y_hbm):
                pltpu.sync_copy(y_hbm.at[i_vmem.at[0]], o_vmem)

            pltpu.emit_pipeline(
                body,
                grid=(n_idx // w,),
                in_specs=[pl.BlockSpec((1, w), lambda j: (0, j))],
                out_specs=[pl.BlockSpec((w, dh), lambda j: (j, 0))],
                core_axis_name=("c", "s"),
                dimension_semantics=(pltpu.PARALLEL,),
            )(i_hbm, o_hbm)

    return run(halves[0], halves[1], pos)


def _experts_kernel(te_ref, na_ref, xa_ref, xb_ref, wg_ref, wu_ref, wd_ref, oa_ref, ob_ref):
    del te_ref

    @pl.when(pl.program_id(0) < na_ref[0])
    def _():
        x = _unpack_rows(xa_ref[...], xb_ref[...]).astype(BF16)
        a = jnp.dot(x, wg_ref[...].astype(BF16), preferred_element_type=F32)
        u = jnp.dot(x, wu_ref[...].astype(BF16), preferred_element_type=F32)
        act = (a * (1.0 / (1.0 + jnp.exp(-a)))) * u
        y = jnp.dot(act.astype(BF16), wd_ref[...].astype(BF16), preferred_element_type=F32)
        oa_ref[...], ob_ref[...] = _pack_rows(y)


def _experts_call(tile_expert, n_active, xs, wg, wu, wd):
    R, dh = xs[0].shape
    tm = TM_MOE
    rows = lambda i, te, na: (jnp.minimum(i, na[0] - 1), 0)
    grid_spec = pltpu.PrefetchScalarGridSpec(
        num_scalar_prefetch=2,
        grid=(R // tm,),
        in_specs=[
            pl.BlockSpec((tm, dh), rows),
            pl.BlockSpec((tm, dh), rows),
            pl.BlockSpec((None, D_MODEL, D_EXPERT), lambda i, te, na: (te[i], 0, 0)),
            pl.BlockSpec((None, D_MODEL, D_EXPERT), lambda i, te, na: (te[i], 0, 0)),
            pl.BlockSpec((None, D_EXPERT, D_MODEL), lambda i, te, na: (te[i], 0, 0)),
        ],
        out_specs=[pl.BlockSpec((tm, dh), rows), pl.BlockSpec((tm, dh), rows)],
    )
    return pl.pallas_call(
        _experts_kernel,
        grid_spec=grid_spec,
        out_shape=[jax.ShapeDtypeStruct((R, dh), U32)] * 2,
        compiler_params=_params(("arbitrary",)),
        name="moe_experts",
    )(tile_expert, n_active, xs[0], xs[1], wg, wu, wd)


def _combine_kernel(x1_ref, y1a_ref, y1b_ref, y2a_ref, y2b_ref, route_ref, mod_ref, o_ref):
    route = route_ref[...]
    lane = lax.broadcasted_iota(jnp.int32, route.shape, 1)
    w1 = jnp.sum(jnp.where(lane == 2, route, 0.0), axis=-1, keepdims=True)
    w2 = jnp.sum(jnp.where(lane == 3, route, 0.0), axis=-1, keepdims=True)
    y1 = _unpack_rows(y1a_ref[...], y1b_ref[...])
    y2 = _unpack_rows(y2a_ref[...], y2b_ref[...])
    o_ref[...] = x1_ref[...] + mod_ref[5:6, :] * (w1 * y1 + w2 * y2)


def _combine_call(x1, ys, route, mod, seq):
    N, D = x1.shape
    tm = TM_OUT
    per_b = seq // tm
    nblk = N // tm
    dh = ys[0].shape[1]
    row = lambda i: (i, 0)
    second = lambda i: (i + nblk, 0)
    return pl.pallas_call(
        _combine_kernel,
        grid=(nblk,),
        in_specs=[
            pl.BlockSpec((tm, D), row),
            pl.BlockSpec((tm, dh), row),
            pl.BlockSpec((tm, dh), row),
            pl.BlockSpec((tm, dh), second),
            pl.BlockSpec((tm, dh), second),
            pl.BlockSpec((tm, LANES), row),
            pl.BlockSpec((None, N_MOD, D), lambda i: (i // per_b, 0, 0)),
        ],
        out_specs=pl.BlockSpec((tm, D), row),
        out_shape=jax.ShapeDtypeStruct((N, D), F32),
        compiler_params=_params(("parallel",)),
        name="moe_combine",
    )(x1, ys[0], ys[1], ys[0], ys[1], route, mod)


def _routing_tables(route_t, counts, n_rows):
    tm = TM_MOE
    cnt = counts[0, :N_EXPERTS].astype(jnp.int32)
    padded = ((cnt + tm - 1) // tm) * tm
    ends = jnp.cumsum(padded)
    base = ends - padded
    expert_ids = jnp.arange(N_EXPERTS, dtype=jnp.int32)[:, None]

    def rows_of(expert, rank):
        first = jnp.sum(jnp.where(expert_ids == expert[None, :], base[:, None], 0), axis=0)
        return first + rank

    rt = route_t.astype(jnp.int32)
    pos = jnp.concatenate([rows_of(rt[0], rt[4]), rows_of(rt[1], rt[5])]).reshape(1, -1)
    n_active = (ends[-1] // tm).reshape(1)
    tile_start = jnp.arange(n_rows // tm, dtype=jnp.int32) * tm
    tile_start = jnp.minimum(tile_start, ends[-1] - tm)
    tile_expert = jnp.sum(tile_start[:, None] >= ends[None, :], axis=1).astype(jnp.int32)
    return pos, tile_expert, n_active


def kernel(x, c, w_ada, b_ada, norm_mix, norm_ffn, w_in, w_gk2, b_gk, q_gain, k_gain, gla_gain, w_out,
           w_router_grp, b_router_grp, w_router_exp, b_router_exp, w_gate, w_up, w_down):
    B, S, D = x.shape
    L = w_ada.shape[0]
    N = B * S

    mod_all = _ada_call(c, w_ada, b_ada).reshape(L, B, N_MOD, D)

    w_in_p = jnp.pad(w_in, ((0, 0), (0, 0), (0, D_IN_PAD - w_in.shape[-1]))).astype(BF16)
    w_gk2_p = jnp.pad(w_gk2, ((0, 0), (0, LANES - GLA_RANK), (0, 0))).astype(BF16)
    qg = jnp.tile(q_gain, (1, SB_HEADS)).reshape(L, 1, SB_WIDTH)
    kg = jnp.tile(k_gain, (1, SB_HEADS)).reshape(L, 1, SB_WIDTH)
    w_out_b = w_out.astype(BF16)
    w_r = jnp.concatenate([w_router_exp, w_router_grp], axis=-1)
    w_r = jnp.pad(w_r, ((0, 0), (0, 0), (0, LANES - w_r.shape[-1])))
    w_r_hi = w_r.astype(BF16)
    w_r_lo = (w_r - w_r_hi.astype(F32)).astype(BF16)
    b_r = jnp.concatenate([b_router_exp, b_router_grp], axis=-1)
    b_r = jnp.pad(b_r, ((0, 0), (0, LANES - b_r.shape[-1]))).reshape(L, 1, LANES)
    wg = w_gate.reshape(L * N_EXPERTS, D, D_EXPERT)
    wu = w_up.reshape(L * N_EXPERTS, D, D_EXPERT)
    wd = w_down.reshape(L * N_EXPERTS, D_EXPERT, D)

    n_rows = 2 * N + N_EXPERTS * TM_MOE
    xf = x.reshape(N, D)
    for l in range(L):
        mod = mod_all[l]
        q, k, v, gq, gk, gv, gr, lg = _proj_call(
            xf, mod, norm_mix[l].reshape(1, D), w_in_p, qg[l], kg[l], w_gk2_p[l],
            b_gk[l].reshape(1, GLA_K_WIDTH), S, l)
        out_a = _sb_call(q, k, v, B, S)
        out_b = _gla_call(gq, gk, gv, gr, lg, gla_gain[l].reshape(1, GLA_DV), B, S)
        x1, h2a, h2b, route, route_t, counts = _out_call(out_a, out_b, xf, mod, norm_ffn[l].reshape(1, D),
                                          w_out_b, w_r_hi, w_r_lo, b_r[l], S, l)
        pos, tile_expert, n_active = _routing_tables(route_t, counts, n_rows)
        xs = _sc_dispatch((h2a, h2b), pos, n_rows)
        ys = _experts_call(tile_expert + l * N_EXPERTS, n_active, xs, wg, wu, wd)
        yt = _sc_collect(ys, pos)
        xf = _combine_call(x1, yt, route, mod, S)
    return xf.reshape(B, S, D)
```

```python
import functools

import jax
import jax.numpy as jnp
from jax import lax
from jax.experimental import pallas as pl
from jax.experimental.pallas import tpu as pltpu
from jax.experimental.pallas import tpu_sc as plsc

F32 = jnp.float32
BF16 = jnp.bfloat16

D_MODEL = 1024
SB_HEADS = 8
SB_HEAD_DIM = 64
SB_WIDTH = 512
GLA_HEADS = 4
GLA_DK = 64
GLA_DV = 128
GLA_K_WIDTH = 256
GLA_V_WIDTH = 512
GLA_RANK = 16
GLA_GATE_NORM = 16.0
GLA_CHUNK = 64
N_GROUPS = 4
EXPERTS_PER_GROUP = 8
N_EXPERTS = N_GROUPS * EXPERTS_PER_GROUP
D_EXPERT = 256
N_MOD = 6
ROUTER_ROWS = 48
EPS = 1e-6

LANES = 128
SUBLANES = 8
VMEM_LIMIT = 56 * 1024 * 1024

_C_Q, _C_K, _C_V = 0, 512, 1024
_C_GQ, _C_GK, _C_GV, _C_GR, _C_LR = 1536, 1792, 2048, 2560, 3072
D_IN_PAD = 3200

TM_PROJ = 512
TM_OUT = 512
TM_MOE = 512
SC_WINDOW = 128
D_PACK = D_MODEL // 4
U32 = jnp.uint32
SB_BQ = 256
SB_PAIRS = 4
GLA_TS = 512
GLA_SUB = 256
LOG2E = 1.4426950408889634
SB_SKIP_THRESH = 160.0


def _params(sem):
    return pltpu.CompilerParams(dimension_semantics=sem, vmem_limit_bytes=VMEM_LIMIT)


def _split_hi_lo(x):
    hi = x.astype(BF16)
    lo = (x - hi.astype(F32)).astype(BF16)
    return hi, lo


def _pack_rows(x):
    r = pltpu.bitcast(x.astype(BF16).astype(F32), U32)
    out = []
    for half in range(2):
        c0 = half * 2 * D_PACK
        out.append(r[:, c0:c0 + D_PACK] | (r[:, c0 + D_PACK:c0 + 2 * D_PACK] >> 16))
    return out


def _unpack_rows(pa, pb):
    parts = []
    for p in (pa, pb):
        parts.append(pltpu.bitcast(p & U32(0xFFFF0000), F32))
        parts.append(pltpu.bitcast(p << 16, F32))
    return jnp.concatenate(parts, axis=1)


def _softplus2(z2):
    neg_abs = pltpu.bitcast(pltpu.bitcast(z2, jnp.uint32) | jnp.uint32(0x80000000), F32)
    return jnp.maximum(z2, 0.0) + jnp.log2(1.0 + jnp.exp2(neg_abs))


def _ada_kernel(c_ref, w_ref, b_ref, o_ref):
    c = c_ref[...]
    ca = (c * (1.0 / (1.0 + jnp.exp(-c)))).astype(BF16)
    o_ref[...] = jnp.dot(ca, w_ref[...].astype(BF16), preferred_element_type=F32) + b_ref[...]


def _ada_call(c, w_ada, b_ada):
    L, D, _ = w_ada.shape
    B = c.shape[0]
    return pl.pallas_call(
        _ada_kernel,
        grid=(L, N_MOD),
        in_specs=[
            pl.BlockSpec((B, D), lambda l, j: (0, 0)),
            pl.BlockSpec((None, D, D), lambda l, j: (l, 0, j)),
            pl.BlockSpec((None, 1, D), lambda l, j: (l, 0, j)),
        ],
        out_specs=pl.BlockSpec((None, B, D), lambda l, j: (l, 0, j)),
        out_shape=jax.ShapeDtypeStruct((L, B, N_MOD * D), F32),
        compiler_params=_params(("parallel", "parallel")),
        name="ada_mod",
    )(c, w_ada, b_ada.reshape(L, 1, N_MOD * D))


def _proj_kernel(x_ref, mod_ref, g_ref, w_ref, qg_ref, kg_ref, wgk_ref, bgk_ref,
                 q_ref, k_ref, v_ref, gq_ref, gk_ref, gv_ref, gr_ref, lg_ref):
    x = x_ref[...]
    shift = mod_ref[0:1, :]
    scale = mod_ref[1:2, :]
    y = x * lax.rsqrt(jnp.mean(x * x, axis=-1, keepdims=True) + EPS) * g_ref[...]
    h = (y * (1.0 + scale) + shift).astype(BF16)

    def proj(c0, width):
        return jnp.dot(h, w_ref[:, c0:c0 + width], preferred_element_type=F32)

    r = lax.broadcasted_iota(jnp.int32, (SB_WIDTH, SB_WIDTH), 0) // SB_HEAD_DIM
    c = lax.broadcasted_iota(jnp.int32, (SB_WIDTH, SB_WIDTH), 1) // SB_HEAD_DIM
    avg = jnp.where(r == c, 1.0 / SB_HEAD_DIM, 0.0).astype(BF16)

    def head_norm(t, gain):
        ms = jnp.dot((t * t).astype(BF16), avg, preferred_element_type=F32)
        return t * lax.rsqrt(ms + EPS) * gain

    q = head_norm(proj(_C_Q, SB_WIDTH), qg_ref[...])
    q_ref[...] = (q * (SB_HEAD_DIM ** -0.5 * LOG2E)).astype(BF16)
    k_ref[...] = head_norm(proj(_C_K, SB_WIDTH), kg_ref[...]).astype(BF16)
    v_ref[...] = proj(_C_V, SB_WIDTH).astype(BF16)
    gq_ref[...] = proj(_C_GQ, GLA_K_WIDTH).astype(BF16)
    gk_ref[...] = proj(_C_GK, GLA_K_WIDTH).astype(BF16)
    gv_ref[...] = proj(_C_GV, GLA_V_WIDTH).astype(BF16)
    gr_ref[...] = proj(_C_GR, GLA_V_WIDTH).astype(BF16)
    lr = proj(_C_LR, LANES).astype(BF16)
    pre = jnp.dot(lr, wgk_ref[...], preferred_element_type=F32) + bgk_ref[...]
    lg_ref[...] = (jnp.minimum(pre, 0.0) - jnp.log(1.0 + jnp.exp(-jnp.abs(pre)))) * (1.0 / GLA_GATE_NORM)


def _proj_call(x, mod, g, w_in, q_gain, k_gain, w_gk2, b_gk, seq, layer):
    N, D = x.shape
    tm = TM_PROJ
    per_b = seq // tm
    row = lambda i: (i, 0)
    const = lambda i: (0, 0)
    widths = [(SB_WIDTH, BF16)] * 3 + [(GLA_K_WIDTH, BF16)] * 2 + [(GLA_V_WIDTH, BF16)] * 2 + [(GLA_K_WIDTH, F32)]
    return pl.pallas_call(
        _proj_kernel,
        grid=(N // tm,),
        in_specs=[
            pl.BlockSpec((tm, D), row),
            pl.BlockSpec((None, N_MOD, D), lambda i: (i // per_b, 0, 0)),
            pl.BlockSpec((1, D), const),
            pl.BlockSpec((None, D, D_IN_PAD), lambda i: (layer, 0, 0)),
            pl.BlockSpec((1, SB_WIDTH), const),
            pl.BlockSpec((1, SB_WIDTH), const),
            pl.BlockSpec((LANES, GLA_K_WIDTH), const),
            pl.BlockSpec((1, GLA_K_WIDTH), const),
        ],
        out_specs=[pl.BlockSpec((tm, w), row) for w, _ in widths],
        out_shape=[jax.ShapeDtypeStruct((N, w), dt) for w, dt in widths],
        compiler_params=_params(("parallel",)),
        name="in_proj",
    )(x, mod, g, w_in, q_gain, k_gain, w_gk2, b_gk)


def _sb_kernel(q_ref, k_ref, v_ref, o_ref):
    i = pl.program_id(2)
    bq = SB_BQ
    lane = lax.broadcasted_iota(jnp.int32, (1, LANES), 1)
    qms = []
    for pr in range(SB_PAIRS):
        q = q_ref[:, pr * LANES:(pr + 1) * LANES]
        for hh in range(2):
            in_head = (lane >= hh * SB_HEAD_DIM) & (lane < (hh + 1) * SB_HEAD_DIM)
            qms.append(jnp.where(in_head, q, jnp.zeros_like(q)))
    nh = len(qms)

    t_loc = lax.broadcasted_iota(jnp.int32, (bq, bq), 0)
    s_loc = lax.broadcasted_iota(jnp.int32, (bq, bq), 1)
    diag_mask = s_loc < t_loc
    later = jnp.where(t_loc > s_loc, 1.0, 0.0).astype(BF16)

    def steps(starts_and_masks, cs, accs):
        pairs = [(s, hh) for s in range(len(starts_and_masks)) for hh in range(nh)]
        block = lambda ref, s, hh: ref[pl.ds(starts_and_masks[s][0], bq), (hh // 2) * LANES:(hh // 2 + 1) * LANES]
        z = {(s, hh): lax.dot_general(qms[hh], block(k_ref, s, hh), (((1,), (1,)), ((), ())),
                                      preferred_element_type=F32) for s, hh in pairs}
        sp = {p: _softplus2(z[p]) for p in pairs}
        lf = {}
        for s, hh in pairs:
            mask = starts_and_masks[s][1]
            lf[s, hh] = (sp[s, hh] if mask is None else jnp.where(mask, sp[s, hh], 0.0)).astype(BF16)
        after = {p: jnp.dot(lf[p], later, preferred_element_type=F32) for p in pairs}
        w = {}
        cs = list(cs)
        for s, hh in pairs:
            mask = starts_and_masks[s][1]
            wt = jnp.exp2(z[s, hh] - sp[s, hh] - after[s, hh] - cs[hh])
            w[s, hh] = (wt if mask is None else jnp.where(mask, wt, 0.0)).astype(BF16)
            cs[hh] = cs[hh] + after[s, hh][:, 0:1] + lf[s, hh][:, 0:1].astype(F32)
        accs = list(accs)
        for s, hh in pairs:
            accs[hh] = accs[hh] + jnp.dot(w[s, hh], block(v_ref, s, hh), preferred_element_type=F32)
        return cs, accs

    c_zero = [jnp.zeros((bq, 1), F32)] * nh
    acc_zero = [jnp.zeros((bq, LANES), F32)] * nh
    diag_start = pl.multiple_of(i * bq, bq)

    def diag_only():
        cs, accs = steps([(diag_start, diag_mask)], c_zero, acc_zero)
        return tuple(cs) + tuple(accs)

    def diag_and_previous():
        cs, accs = steps([(diag_start, diag_mask), (pl.multiple_of((i - 1) * bq, bq), None)], c_zero, acc_zero)
        return tuple(cs) + tuple(accs)

    state0 = lax.cond(i == 0, diag_only, diag_and_previous)

    def cond(state):
        c_min = functools.reduce(jnp.minimum, state[1:1 + nh])
        return jnp.logical_and(state[0] >= 0, jnp.min(c_min) < SB_SKIP_THRESH)

    def body(state):
        j = state[0]
        cs, accs = steps([(pl.multiple_of(j * bq, bq), None)], state[1:1 + nh], state[1 + nh:])
        return (j - 1,) + tuple(cs) + tuple(accs)

    final = lax.while_loop(cond, body, (i - 2,) + tuple(state0))
    accs = final[1 + nh:]
    for pr in range(SB_PAIRS):
        o_ref[:, pr * LANES:(pr + 1) * LANES] = jnp.where(
            lane < SB_HEAD_DIM, accs[2 * pr], accs[2 * pr + 1]).astype(BF16)


def _sb_call(q, k, v, batch, seq):
    N = q.shape[0]
    nq = seq // SB_BQ
    width = SB_PAIRS * LANES
    return pl.pallas_call(
        _sb_kernel,
        grid=(batch, SB_WIDTH // width, nq),
        in_specs=[
            pl.BlockSpec((SB_BQ, width), lambda b, p, i: (b * nq + i, p)),
            pl.BlockSpec((seq, width), lambda b, p, i: (b, p)),
            pl.BlockSpec((seq, width), lambda b, p, i: (b, p)),
        ],
        out_specs=pl.BlockSpec((SB_BQ, width), lambda b, p, i: (b * nq + i, p)),
        out_shape=jax.ShapeDtypeStruct((N, SB_WIDTH), BF16),
        compiler_params=_params(("parallel", "parallel", "arbitrary")),
        name="sb_attn",
    )(q, k, v)


def _gla_kernel(gq_ref, gk_ref, gv_ref, gr_ref, lg_ref, gain_ref, o_ref, st_ref):
    C, SUB, H = GLA_CHUNK, GLA_SUB, GLA_HEADS
    nt = (((1,), (1,)), ((), ()))

    @pl.when(pl.program_id(1) == 0)
    def _():
        st_ref[...] = jnp.zeros_like(st_ref)

    rows = lax.broadcasted_iota(jnp.int32, (SUB, SUB), 0)
    cols = lax.broadcasted_iota(jnp.int32, (SUB, SUB), 1)
    same_chunk = (rows // C) == (cols // C)
    causal = same_chunk & (rows >= cols)
    low = jnp.where(causal, 1.0, 0.0).astype(BF16)
    lane = lax.broadcasted_iota(jnp.int32, (1, GLA_K_WIDTH), 1)
    in_head = [(lane >= h * GLA_DK) & (lane < (h + 1) * GLA_DK) for h in range(H)]
    gain = gain_ref[...]

    st = st_ref[...]
    for sb in range(GLA_TS // SUB):
        r0 = sb * SUB
        hi, lo = _split_hi_lo(lg_ref[r0:r0 + SUB, :])
        b = jnp.dot(low, hi, preferred_element_type=F32) + jnp.dot(low, lo, preferred_element_type=F32)
        b_last = jnp.concatenate(
            [jnp.broadcast_to(b[c0 + C - 1:c0 + C, :], (C, GLA_K_WIDTH)) for c0 in range(0, SUB, C)], axis=0)
        gq = gq_ref[r0:r0 + SUB, :].astype(F32)
        gk = gk_ref[r0:r0 + SUB, :].astype(F32)
        q_dec = gq * jnp.exp(b) * (GLA_DK ** -0.5)
        k_inv = (gk * jnp.exp(-b)).astype(BF16)
        k_dec = (gk * jnp.exp(b_last - b)).astype(BF16)
        decay = jnp.exp(b_last)
        qh = [jnp.where(in_head[h], q_dec, 0.0).astype(BF16) for h in range(H)]
        kh = [jnp.where(in_head[h], k_dec, jnp.zeros_like(k_dec)) for h in range(H)]

        intra = []
        for h in range(H):
            s = lax.dot_general(qh[h], k_inv, nt, preferred_element_type=F32)
            s = jnp.where(causal, s, 0.0).astype(BF16)
            vh = gv_ref[r0:r0 + SUB, h * GLA_DV:(h + 1) * GLA_DV]
            intra.append(jnp.dot(s, vh, preferred_element_type=F32))

        kvs = []
        for c0 in range(0, SUB, C):
            kv = None
            for h in range(H):
                v_t = gv_ref[r0 + c0:r0 + c0 + C, h * GLA_DV:(h + 1) * GLA_DV].astype(F32).T.astype(BF16)
                part = jnp.dot(v_t, kh[h][c0:c0 + C], preferred_element_type=F32)
                kv = part if kv is None else kv + part
            kvs.append(kv)
        states = []
        for c, c0 in enumerate(range(0, SUB, C)):
            states.append(st.astype(BF16))
            st = st * decay[c0:c0 + 1, :] + kvs[c]
        inter = [[] for _ in range(H)]
        for c, c0 in enumerate(range(0, SUB, C)):
            for h in range(H):
                inter[h].append(lax.dot_general(qh[h][c0:c0 + C], states[c], nt, preferred_element_type=F32))

        for h in range(H):
            o = intra[h] + jnp.concatenate(inter[h], axis=0)
            y = o * lax.rsqrt(jnp.mean(o * o, axis=-1, keepdims=True) + EPS) * gain
            g = gr_ref[r0:r0 + SUB, h * GLA_DV:(h + 1) * GLA_DV].astype(F32)
            y = y * (g * (1.0 / (1.0 + jnp.exp(-g))))
            o_ref[r0:r0 + SUB, h * GLA_DV:(h + 1) * GLA_DV] = y.astype(BF16)
    st_ref[...] = st


def _gla_call(gq, gk, gv, gr, lg, gain, batch, seq):
    N = gq.shape[0]
    nt = seq // GLA_TS
    row = lambda b, t: (b * nt + t, 0)
    return pl.pallas_call(
        _gla_kernel,
        grid=(batch, nt),
        in_specs=[
            pl.BlockSpec((GLA_TS, GLA_K_WIDTH), row),
            pl.BlockSpec((GLA_TS, GLA_K_WIDTH), row),
            pl.BlockSpec((GLA_TS, GLA_V_WIDTH), row),
            pl.BlockSpec((GLA_TS, GLA_V_WIDTH), row),
            pl.BlockSpec((GLA_TS, GLA_K_WIDTH), row),
            pl.BlockSpec((1, GLA_DV), lambda b, t: (0, 0)),
        ],
        out_specs=pl.BlockSpec((GLA_TS, GLA_V_WIDTH), row),
        out_shape=jax.ShapeDtypeStruct((N, GLA_V_WIDTH), BF16),
        scratch_shapes=[pltpu.VMEM((GLA_DV, GLA_K_WIDTH), F32)],
        compiler_params=_params(("parallel", "arbitrary")),
        name="gla",
    )(gq, gk, gv, gr, lg, gain)


def _out_kernel(a_ref, b_ref, x_ref, mod_ref, g_ref, wa_ref, wb_ref, wrh_ref, wrl_ref, br_ref,
                x1_ref, h2a_ref, h2b_ref, route_ref, route_t_ref, cnt_ref, run_ref):
    @pl.when(pl.program_id(0) == 0)
    def _():
        run_ref[...] = jnp.zeros_like(run_ref)

    gate_m = mod_ref[2:3, :]
    shift_f = mod_ref[3:4, :]
    scale_f = mod_ref[4:5, :]
    y = jnp.dot(a_ref[...], wa_ref[...], preferred_element_type=F32)
    y = y + jnp.dot(b_ref[...], wb_ref[...], preferred_element_type=F32)
    x1 = x_ref[...] + gate_m * y
    x1_ref[...] = x1
    n = x1 * lax.rsqrt(jnp.mean(x1 * x1, axis=-1, keepdims=True) + EPS) * g_ref[...]
    h2 = n * (1.0 + scale_f) + shift_f
    h_hi, h_lo = _split_hi_lo(h2)
    h2a_ref[...], h2b_ref[...] = _pack_rows(h2)

    nt = (((1,), (1,)), ((), ()))
    wr_hi = wrh_ref[...]
    lt = lax.dot_general(wr_hi, h_hi, nt, preferred_element_type=F32)
    lt = lt + lax.dot_general(wr_hi, h_lo, nt, preferred_element_type=F32)
    lt = lt + lax.dot_general(wrl_ref[...], h_hi, nt, preferred_element_type=F32)
    lt = lt + br_ref[...]
    tm = lt.shape[1]
    row_of = lambda r: lt[r:r + 1, :]

    def first_argmax(vals):
        best = functools.reduce(jnp.maximum, vals)
        idx = jnp.full((1, tm), len(vals) - 1, jnp.int32)
        for j in reversed(range(len(vals) - 1)):
            idx = jnp.where(vals[j] == best, j, idx)
        return best, idx

    glog = [row_of(N_EXPERTS + j) for j in range(N_GROUPS)]
    gmax, gidx = first_argmax(glog)
    grp_w = 1.0 / functools.reduce(jnp.add, [jnp.exp(v - gmax) for v in glog])

    elog = []
    for e in range(EXPERTS_PER_GROUP):
        v = row_of((N_GROUPS - 1) * EXPERTS_PER_GROUP + e)
        for j in reversed(range(N_GROUPS - 1)):
            v = jnp.where(gidx == j, row_of(j * EXPERTS_PER_GROUP + e), v)
        elog.append(v)
    m1, i1 = first_argmax(elog)
    neg = jnp.float32(-jnp.inf)
    m2, i2 = first_argmax([jnp.where(i1 == e, neg, v) for e, v in enumerate(elog)])
    e21 = jnp.exp(m2 - m1)
    p1 = 1.0 / (1.0 + e21)
    p2 = e21 * p1
    ea = gidx * EXPERTS_PER_GROUP + i1
    eb = gidx * EXPERTS_PER_GROUP + i2

    rows = lax.broadcasted_iota(jnp.int32, (N_EXPERTS, tm), 0)
    sel_a = rows == ea
    sel_b = rows == eb
    onehot = jnp.where(sel_a | sel_b, 1.0, 0.0)
    rr = lax.broadcasted_iota(jnp.int32, (tm, tm), 0)
    cc = lax.broadcasted_iota(jnp.int32, (tm, tm), 1)
    earlier = jnp.where(rr < cc, 1.0, 0.0).astype(BF16)
    seen = jnp.dot(onehot.astype(BF16), earlier, preferred_element_type=F32) + run_ref[:, 0:1]
    rank_a = jnp.sum(jnp.where(sel_a, seen, 0.0), axis=0, keepdims=True)
    rank_b = jnp.sum(jnp.where(sel_b, seen, 0.0), axis=0, keepdims=True)
    run = run_ref[...] + jnp.sum(onehot, axis=1, keepdims=True)
    run_ref[...] = run
    cnt_ref[...] = run

    fields = (ea.astype(F32), eb.astype(F32), p1 * grp_w, p2 * grp_w, rank_a, rank_b)
    field_row = lax.broadcasted_iota(jnp.int32, (LANES, tm), 0)
    route_t = jnp.zeros((LANES, tm), F32)
    for idx, val in enumerate(fields):
        route_t = jnp.where(field_row == idx, val, route_t)
    route_t_ref[...] = route_t[:SUBLANES, :]
    route_ref[...] = route_t.T


def _out_call(oa, ob, x, mod, g, w_out, wr_hi, wr_lo, br, seq, layer):
    N, D = x.shape
    tm = TM_OUT
    per_b = seq // tm
    row = lambda i: (i, 0)
    const = lambda i: (0, 0)
    return pl.pallas_call(
        _out_kernel,
        grid=(N // tm,),
        in_specs=[
            pl.BlockSpec((tm, SB_WIDTH), row),
            pl.BlockSpec((tm, GLA_V_WIDTH), row),
            pl.BlockSpec((tm, D), row),
            pl.BlockSpec((None, N_MOD, D), lambda i: (i // per_b, 0, 0)),
            pl.BlockSpec((1, D), const),
            pl.BlockSpec((None, SB_WIDTH, D), lambda i: (layer, 0, 0)),
            pl.BlockSpec((None, GLA_V_WIDTH, D), lambda i: (layer, 1, 0)),
            pl.BlockSpec((None, ROUTER_ROWS, D), lambda i: (layer, 0, 0)),
            pl.BlockSpec((None, ROUTER_ROWS, D), lambda i: (layer, 0, 0)),
            pl.BlockSpec((None, ROUTER_ROWS, 1), lambda i: (layer, 0, 0)),
        ],
        out_specs=[pl.BlockSpec((tm, D), row), pl.BlockSpec((tm, D_PACK), row), pl.BlockSpec((tm, D_PACK), row),
                   pl.BlockSpec((tm, LANES), row), pl.BlockSpec((SUBLANES, tm), lambda i: (0, i)),
                   pl.BlockSpec((N_EXPERTS, LANES), const)],
        out_shape=[jax.ShapeDtypeStruct((N, D), F32), jax.ShapeDtypeStruct((N, D_PACK), U32),
                   jax.ShapeDtypeStruct((N, D_PACK), U32), jax.ShapeDtypeStruct((N, LANES), F32),
                   jax.ShapeDtypeStruct((SUBLANES, N), F32), jax.ShapeDtypeStruct((N_EXPERTS, LANES), F32)],
        scratch_shapes=[pltpu.VMEM((N_EXPERTS, LANES), F32)],
        compiler_params=_params(("arbitrary",)),
        name="out_proj_router",
    )(oa, ob, x, mod, g, w_out, w_out, wr_hi, wr_lo, br)


def _sc_mesh():
    return plsc.VectorSubcoreMesh(core_axis_name="c", subcore_axis_name="s")


def _sc_dispatch(halves, pos, n_rows):
    N, dh = halves[0].shape
    w = SC_WINDOW
    nblk = N // w
    out_type = [jax.ShapeDtypeStruct((n_rows, dh), h.dtype) for h in halves]

    @pl.kernel(out_type=out_type, mesh=_sc_mesh(), scratch_types=[], name="moe_dispatch")
    def run(xa_hbm, xb_hbm, i_hbm, oa_hbm, ob_hbm):
        for x_hbm, o_hbm in ((xa_hbm, oa_hbm), (xb_hbm, ob_hbm)):
            def body(x_vmem, i_vmem, o_hbm=o_hbm):
                pltpu.sync_copy(x_vmem, o_hbm.at[i_vmem.at[0]])

            pltpu.emit_pipeline(
                body,
                grid=(2 * nblk,),
                in_specs=[pl.BlockSpec((w, dh), lambda j: (j % nblk, 0)),
                          pl.BlockSpec((1, w), lambda j: (0, j))],
                out_specs=[],
                core_axis_name=("c", "s"),
                dimension_semantics=(pltpu.PARALLEL,),
            )(x_hbm, i_hbm)

    return run(halves[0], halves[1], pos)


def _sc_collect(halves, pos):
    dh = halves[0].shape[1]
    n_idx = pos.shape[1]
    w = SC_WINDOW
    out_type = [jax.ShapeDtypeStruct((n_idx, dh), h.dtype) for h in halves]

    @pl.kernel(out_type=out_type, mesh=_sc_mesh(), scratch_types=[], name="moe_collect")
    def run(ya_hbm, yb_hbm, i_hbm, oa_hbm, ob_hbm):
        for y_hbm, o_hbm in ((ya_hbm, oa_hbm), (yb_hbm, ob_hbm)):
            def body(i_vmem, o_vmem, y_hbm=y_hbm):
                pltpu.sync_copy(y_hbm.at[i_vmem.at[0]], o_vmem)

            pltpu.emit_pipeline(
                body,
                grid=(n_idx // w,),
                in_specs=[pl.BlockSpec((1, w), lambda j: (0, j))],
                out_specs=[pl.BlockSpec((w, dh), lambda j: (j, 0))],
                core_axis_name=("c", "s"),
                dimension_semantics=(pltpu.PARALLEL,),
            )(i_hbm, o_hbm)

    return run(halves[0], halves[1], pos)


def _experts_kernel(te_ref, na_ref, xa_ref, xb_ref, wg_ref, wu_ref, wd_ref, oa_ref, ob_ref):
    del te_ref

    @pl.when(pl.program_id(0) < na_ref[0])
    def _():
        x = _unpack_rows(xa_ref[...], xb_ref[...]).astype(BF16)
        a = jnp.dot(x, wg_ref[...].astype(BF16), preferred_element_type=F32)
        u = jnp.dot(x, wu_ref[...].astype(BF16), preferred_element_type=F32)
        act = (a * (1.0 / (1.0 + jnp.exp(-a)))) * u
        y = jnp.dot(act.astype(BF16), wd_ref[...].astype(BF16), preferred_element_type=F32)
        oa_ref[...], ob_ref[...] = _pack_rows(y)


def _experts_call(tile_expert, n_active, xs, wg, wu, wd):
    R, dh = xs[0].shape
    tm = TM_MOE
    rows = lambda i, te, na: (jnp.minimum(i, na[0] - 1), 0)
    grid_spec = pltpu.PrefetchScalarGridSpec(
        num_scalar_prefetch=2,
        grid=(R // tm,),
        in_specs=[
            pl.BlockSpec((tm, dh), rows),
            pl.BlockSpec((tm, dh), rows),
            pl.BlockSpec((None, D_MODEL, D_EXPERT), lambda i, te, na: (te[i], 0, 0)),
            pl.BlockSpec((None, D_MODEL, D_EXPERT), lambda i, te, na: (te[i], 0, 0)),
            pl.BlockSpec((None, D_EXPERT, D_MODEL), lambda i, te, na: (te[i], 0, 0)),
        ],
        out_specs=[pl.BlockSpec((tm, dh), rows), pl.BlockSpec((tm, dh), rows)],
    )
    return pl.pallas_call(
        _experts_kernel,
        grid_spec=grid_spec,
        out_shape=[jax.ShapeDtypeStruct((R, dh), U32)] * 2,
        compiler_params=_params(("arbitrary",)),
        name="moe_experts",
    )(tile_expert, n_active, xs[0], xs[1], wg, wu, wd)


def _combine_kernel(x1_ref, y1a_ref, y1b_ref, y2a_ref, y2b_ref, route_ref, mod_ref, o_ref):
    route = route_ref[...]
    lane = lax.broadcasted_iota(jnp.int32, route.shape, 1)
    w1 = jnp.sum(jnp.where(lane == 2, route, 0.0), axis=-1, keepdims=True)
    w2 = jnp.sum(jnp.where(lane == 3, route, 0.0), axis=-1, keepdims=True)
    y1 = _unpack_rows(y1a_ref[...], y1b_ref[...])
    y2 = _unpack_rows(y2a_ref[...], y2b_ref[...])
    o_ref[...] = x1_ref[...] + mod_ref[5:6, :] * (w1 * y1 + w2 * y2)


def _combine_call(x1, ys, route, mod, seq):
    N, D = x1.shape
    tm = TM_OUT
    per_b = seq // tm
    nblk = N // tm
    dh = ys[0].shape[1]
    row = lambda i: (i, 0)
    second = lambda i: (i + nblk, 0)
    return pl.pallas_call(
        _combine_kernel,
        grid=(nblk,),
        in_specs=[
            pl.BlockSpec((tm, D), row),
            pl.BlockSpec((tm, dh), row),
            pl.BlockSpec((tm, dh), row),
            pl.BlockSpec((tm, dh), second),
            pl.BlockSpec((tm, dh), second),
            pl.BlockSpec((tm, LANES), row),
            pl.BlockSpec((None, N_MOD, D), lambda i: (i // per_b, 0, 0)),
        ],
        out_specs=pl.BlockSpec((tm, D), row),
        out_shape=jax.ShapeDtypeStruct((N, D), F32),
        compiler_params=_params(("parallel",)),
        name="moe_combine",
    )(x1, ys[0], ys[1], ys[0], ys[1], route, mod)


def _routing_tables(route_t, counts, n_rows):
    tm = TM_MOE
    cnt = counts[:, 0].astype(jnp.int32)
    padded = ((cnt + tm - 1) // tm) * tm
    ends = jnp.cumsum(padded)
    base = ends - padded
    expert_ids = jnp.arange(N_EXPERTS, dtype=jnp.int32)[:, None]

    def rows_of(expert, rank):
        first = jnp.sum(jnp.where(expert_ids == expert[None, :], base[:, None], 0), axis=0)
        return first + rank

    rt = route_t.astype(jnp.int32)
    pos = jnp.concatenate([rows_of(rt[0], rt[4]), rows_of(rt[1], rt[5])]).reshape(1, -1)
    n_active = (ends[-1] // tm).reshape(1)
    tile_start = jnp.arange(n_rows // tm, dtype=jnp.int32) * tm
    tile_start = jnp.minimum(tile_start, ends[-1] - tm)
    tile_expert = jnp.sum(tile_start[:, None] >= ends[None, :], axis=1).astype(jnp.int32)
    return pos, tile_expert, n_active


def kernel(x, c, w_ada, b_ada, norm_mix, norm_ffn, w_in, w_gk2, b_gk, q_gain, k_gain, gla_gain, w_out,
           w_router_grp, b_router_grp, w_router_exp, b_router_exp, w_gate, w_up, w_down):
    B, S, D = x.shape
    L = w_ada.shape[0]
    N = B * S

    mod_all = _ada_call(c, w_ada, b_ada).reshape(L, B, N_MOD, D)

    w_in_p = jnp.pad(w_in, ((0, 0), (0, 0), (0, D_IN_PAD - w_in.shape[-1]))).astype(BF16)
    w_gk2_p = jnp.pad(w_gk2, ((0, 0), (0, LANES - GLA_RANK), (0, 0))).astype(BF16)
    qg = jnp.tile(q_gain, (1, SB_HEADS)).reshape(L, 1, SB_WIDTH)
    kg = jnp.tile(k_gain, (1, SB_HEADS)).reshape(L, 1, SB_WIDTH)
    w_out_b = w_out.astype(BF16)
    w_r = jnp.swapaxes(jnp.concatenate([w_router_exp, w_router_grp], axis=-1), 1, 2)
    w_r = jnp.pad(w_r, ((0, 0), (0, ROUTER_ROWS - w_r.shape[1]), (0, 0)))
    w_r_hi = w_r.astype(BF16)
    w_r_lo = (w_r - w_r_hi.astype(F32)).astype(BF16)
    b_r = jnp.concatenate([b_router_exp, b_router_grp], axis=-1)
    b_r = jnp.pad(b_r, ((0, 0), (0, ROUTER_ROWS - b_r.shape[-1]))).reshape(L, ROUTER_ROWS, 1)
    wg = w_gate.reshape(L * N_EXPERTS, D, D_EXPERT)
    wu = w_up.reshape(L * N_EXPERTS, D, D_EXPERT)
    wd = w_down.reshape(L * N_EXPERTS, D_EXPERT, D)

    n_rows = 2 * N + N_EXPERTS * TM_MOE
    xf = x.reshape(N, D)
    for l in range(L):
        mod = mod_all[l]
        q, k, v, gq, gk, gv, gr, lg = _proj_call(
            xf, mod, norm_mix[l].reshape(1, D), w_in_p, qg[l], kg[l], w_gk2_p[l],
            b_gk[l].reshape(1, GLA_K_WIDTH), S, l)
        out_a = _sb_call(q, k, v, B, S)
        out_b = _gla_call(gq, gk, gv, gr, lg, gla_gain[l].reshape(1, GLA_DV), B, S)
        x1, h2a, h2b, route, route_t, counts = _out_call(out_a, out_b, xf, mod, norm_ffn[l].reshape(1, D),
                                          w_out_b, w_r_hi, w_r_lo, b_r, S, l)
        pos, tile_expert, n_active = _routing_tables(route_t, counts, n_rows)
        xs = _sc_dispatch((h2a, h2b), pos, n_rows)
        ys = _experts_call(tile_expert + l * N_EXPERTS, n_active, xs, wg, wu, wd)
        yt = _sc_collect(ys, pos)
        xf = _combine_call(x1, yt, route, mod, S)
    return xf.reshape(B, S, D)
```

```python
import functools

import jax
import jax.numpy as jnp
from jax import lax
from jax.experimental import pallas as pl
from jax.experimental.pallas import tpu as pltpu
from jax.experimental.pallas import tpu_sc as plsc

F32 = jnp.float32
BF16 = jnp.bfloat16

D_MODEL = 1024
SB_HEADS = 8
SB_HEAD_DIM = 64
SB_WIDTH = 512
GLA_HEADS = 4
GLA_DK = 64
GLA_DV = 128
GLA_K_WIDTH = 256
GLA_V_WIDTH = 512
GLA_RANK = 16
GLA_GATE_NORM = 16.0
GLA_CHUNK = 64
N_GROUPS = 4
EXPERTS_PER_GROUP = 8
N_EXPERTS = N_GROUPS * EXPERTS_PER_GROUP
D_EXPERT = 256
N_MOD = 6
ROUTER_ROWS = 48
EPS = 1e-6

LANES = 128
SUBLANES = 8
VMEM_LIMIT = 56 * 1024 * 1024

_C_Q, _C_K, _C_V = 0, 512, 1024
_C_GQ, _C_GK, _C_GV, _C_GR, _C_LR = 1536, 1792, 2048, 2560, 3072
D_IN_PAD = 3200

TM_PROJ = 512
TM_OUT = 512
TM_MOE = 512
SC_WINDOW = 128
D_PACK = D_MODEL // 4
U32 = jnp.uint32
SB_BQ = 256
SB_PAIRS = 4
SB_GROUP = 2
GLA_TS = 512
GLA_SUB = 256
LOG2E = 1.4426950408889634
SB_SKIP_THRESH = 160.0


def _params(sem):
    return pltpu.CompilerParams(dimension_semantics=sem, vmem_limit_bytes=VMEM_LIMIT)


def _split_hi_lo(x):
    hi = x.astype(BF16)
    lo = (x - hi.astype(F32)).astype(BF16)
    return hi, lo


def _pack_rows(x):
    r = pltpu.bitcast(x.astype(BF16).astype(F32), U32)
    out = []
    for half in range(2):
        c0 = half * 2 * D_PACK
        out.append(r[:, c0:c0 + D_PACK] | (r[:, c0 + D_PACK:c0 + 2 * D_PACK] >> 16))
    return out


def _unpack_rows(pa, pb):
    parts = []
    for p in (pa, pb):
        parts.append(pltpu.bitcast(p & U32(0xFFFF0000), F32))
        parts.append(pltpu.bitcast(p << 16, F32))
    return jnp.concatenate(parts, axis=1)


def _softplus2(z2):
    neg_abs = pltpu.bitcast(pltpu.bitcast(z2, jnp.uint32) | jnp.uint32(0x80000000), F32)
    return jnp.maximum(z2, 0.0) + jnp.log2(1.0 + jnp.exp2(neg_abs))


def _ada_kernel(c_ref, w_ref, b_ref, o_ref):
    c = c_ref[...]
    ca = (c * (1.0 / (1.0 + jnp.exp(-c)))).astype(BF16)
    o_ref[...] = jnp.dot(ca, w_ref[...].astype(BF16), preferred_element_type=F32) + b_ref[...]


def _ada_call(c, w_ada, b_ada):
    L, D, _ = w_ada.shape
    B = c.shape[0]
    return pl.pallas_call(
        _ada_kernel,
        grid=(L, N_MOD),
        in_specs=[
            pl.BlockSpec((B, D), lambda l, j: (0, 0)),
            pl.BlockSpec((None, D, D), lambda l, j: (l, 0, j)),
            pl.BlockSpec((None, 1, D), lambda l, j: (l, 0, j)),
        ],
        out_specs=pl.BlockSpec((None, B, D), lambda l, j: (l, 0, j)),
        out_shape=jax.ShapeDtypeStruct((L, B, N_MOD * D), F32),
        compiler_params=_params(("parallel", "parallel")),
        name="ada_mod",
    )(c, w_ada, b_ada.reshape(L, 1, N_MOD * D))


def _moe_residual(x1_ref, y1a_ref, y1b_ref, y2a_ref, y2b_ref, route_ref, mod_ref):
    route = route_ref[...]
    lane = lax.broadcasted_iota(jnp.int32, route.shape, 1)
    w1 = jnp.sum(jnp.where(lane == 2, route, 0.0), axis=-1, keepdims=True)
    w2 = jnp.sum(jnp.where(lane == 3, route, 0.0), axis=-1, keepdims=True)
    y1 = _unpack_rows(y1a_ref[...], y1b_ref[...])
    y2 = _unpack_rows(y2a_ref[...], y2b_ref[...])
    return x1_ref[...] + mod_ref[5:6, :] * (w1 * y1 + w2 * y2)


def _proj_kernel(after_moe, *refs):
    if after_moe:
        moe_refs, refs = refs[:7], refs[7:]
    else:
        x_ref, refs = refs[0], refs[1:]
    mod_ref, g_ref, w_ref, qg_ref, kg_ref, wgk_ref, bgk_ref = refs[:7]
    outs = refs[7:]
    if after_moe:
        x = _moe_residual(*moe_refs)
        outs[0][...] = x
        outs = outs[1:]
    else:
        x = x_ref[...]
    q_ref, k_ref, v_ref, gq_ref, gk_ref, gv_ref, gr_ref, lg_ref = outs
    shift = mod_ref[0:1, :]
    scale = mod_ref[1:2, :]
    y = x * lax.rsqrt(jnp.mean(x * x, axis=-1, keepdims=True) + EPS) * g_ref[...]
    h = (y * (1.0 + scale) + shift).astype(BF16)

    def proj(c0, width):
        return jnp.dot(h, w_ref[:, c0:c0 + width], preferred_element_type=F32)

    r = lax.broadcasted_iota(jnp.int32, (SB_WIDTH, SB_WIDTH), 0) // SB_HEAD_DIM
    c = lax.broadcasted_iota(jnp.int32, (SB_WIDTH, SB_WIDTH), 1) // SB_HEAD_DIM
    avg = jnp.where(r == c, 1.0 / SB_HEAD_DIM, 0.0).astype(BF16)

    def head_norm(t, gain):
        ms = jnp.dot((t * t).astype(BF16), avg, preferred_element_type=F32)
        return t * lax.rsqrt(ms + EPS) * gain

    q = head_norm(proj(_C_Q, SB_WIDTH), qg_ref[...])
    q_ref[...] = (q * (SB_HEAD_DIM ** -0.5 * LOG2E)).astype(BF16)
    k_ref[...] = head_norm(proj(_C_K, SB_WIDTH), kg_ref[...]).astype(BF16)
    v_ref[...] = proj(_C_V, SB_WIDTH).astype(BF16)
    gq_ref[...] = proj(_C_GQ, GLA_K_WIDTH).astype(BF16)
    gk_ref[...] = proj(_C_GK, GLA_K_WIDTH).astype(BF16)
    gv_ref[...] = proj(_C_GV, GLA_V_WIDTH).astype(BF16)
    gr_ref[...] = proj(_C_GR, GLA_V_WIDTH).astype(BF16)
    lr = proj(_C_LR, LANES).astype(BF16)
    pre = jnp.dot(lr, wgk_ref[...], preferred_element_type=F32) + bgk_ref[...]
    lg_ref[...] = (jnp.minimum(pre, 0.0) - jnp.log(1.0 + jnp.exp(-jnp.abs(pre)))) * (1.0 / GLA_GATE_NORM)


def _proj_call(x, moe, mod, g, w_in, q_gain, k_gain, w_gk2, b_gk, seq, layer):
    N, D = x.shape
    tm = TM_PROJ
    per_b = seq // tm
    nblk = N // tm
    row = lambda i: (i, 0)
    const = lambda i: (0, 0)
    mod_spec = pl.BlockSpec((None, N_MOD, D), lambda i: (i // per_b, 0, 0))
    widths = [(SB_WIDTH, BF16)] * 3 + [(GLA_K_WIDTH, BF16)] * 2 + [(GLA_V_WIDTH, BF16)] * 2 + [(GLA_K_WIDTH, F32)]
    x_args, x_specs = [x], [pl.BlockSpec((tm, D), row)]
    if moe is not None:
        ys, route, mod_prev = moe
        dh = ys[0].shape[1]
        second = lambda i: (i + nblk, 0)
        x_args += [ys[0], ys[1], ys[0], ys[1], route, mod_prev]
        x_specs += [pl.BlockSpec((tm, dh), row), pl.BlockSpec((tm, dh), row), pl.BlockSpec((tm, dh), second),
                    pl.BlockSpec((tm, dh), second), pl.BlockSpec((tm, LANES), row), mod_spec]
        widths = [(D, F32)] + widths
    return pl.pallas_call(
        functools.partial(_proj_kernel, moe is not None),
        grid=(nblk,),
        in_specs=x_specs + [
            mod_spec,
            pl.BlockSpec((1, D), const),
            pl.BlockSpec((None, D, D_IN_PAD), lambda i: (layer, 0, 0)),
            pl.BlockSpec((1, SB_WIDTH), const),
            pl.BlockSpec((1, SB_WIDTH), const),
            pl.BlockSpec((LANES, GLA_K_WIDTH), const),
            pl.BlockSpec((1, GLA_K_WIDTH), const),
        ],
        out_specs=[pl.BlockSpec((tm, w), row) for w, _ in widths],
        out_shape=[jax.ShapeDtypeStruct((N, w), dt) for w, dt in widths],
        compiler_params=_params(("parallel",)),
        name="in_proj",
    )(*x_args, mod, g, w_in, q_gain, k_gain, w_gk2, b_gk)


def _sb_kernel(q_ref, k_ref, v_ref, o_ref):
    i = pl.program_id(2)
    bq = SB_BQ
    lane = lax.broadcasted_iota(jnp.int32, (1, LANES), 1)
    qms = []
    for pr in range(SB_PAIRS):
        q = q_ref[:, pr * LANES:(pr + 1) * LANES]
        for hh in range(2):
            in_head = (lane >= hh * SB_HEAD_DIM) & (lane < (hh + 1) * SB_HEAD_DIM)
            qms.append(jnp.where(in_head, q, jnp.zeros_like(q)))
    nh = len(qms)

    t_loc = lax.broadcasted_iota(jnp.int32, (bq, bq), 0)
    s_loc = lax.broadcasted_iota(jnp.int32, (bq, bq), 1)
    diag_mask = s_loc < t_loc
    later = jnp.where(t_loc > s_loc, 1.0, 0.0).astype(BF16)

    def steps(starts_and_masks, cs, accs):
        cs, accs = list(cs), list(accs)
        for h0 in range(0, nh, SB_GROUP):
            stage_group(starts_and_masks, range(h0, h0 + SB_GROUP), cs, accs)
        return cs, accs

    def stage_group(starts_and_masks, heads, cs, accs):
        pairs = [(s, hh) for s in range(len(starts_and_masks)) for hh in heads]
        block = lambda ref, s, hh: ref[pl.ds(starts_and_masks[s][0], bq), (hh // 2) * LANES:(hh // 2 + 1) * LANES]
        z = {(s, hh): lax.dot_general(qms[hh], block(k_ref, s, hh), (((1,), (1,)), ((), ())),
                                      preferred_element_type=F32) for s, hh in pairs}
        lf, logsig = {}, {}
        for s, hh in pairs:
            mask = starts_and_masks[s][1]
            sp = _softplus2(z[s, hh])
            logsig[s, hh] = z[s, hh] - sp
            lf[s, hh] = (sp if mask is None else jnp.where(mask, sp, 0.0)).astype(BF16)
        after = {p: jnp.dot(lf[p], later, preferred_element_type=F32) for p in pairs}
        w = {}
        for s, hh in pairs:
            mask = starts_and_masks[s][1]
            wt = jnp.exp2(logsig[s, hh] - after[s, hh] - cs[hh])
            w[s, hh] = (wt if mask is None else jnp.where(mask, wt, 0.0)).astype(BF16)
            cs[hh] = cs[hh] + after[s, hh][:, 0:1] + lf[s, hh][:, 0:1].astype(F32)
        for s, hh in pairs:
            accs[hh] = accs[hh] + jnp.dot(w[s, hh], block(v_ref, s, hh), preferred_element_type=F32)

    c_zero = [jnp.zeros((bq, 1), F32)] * nh
    acc_zero = [jnp.zeros((bq, LANES), F32)] * nh
    diag_start = pl.multiple_of(i * bq, bq)

    def diag_only():
        cs, accs = steps([(diag_start, diag_mask)], c_zero, acc_zero)
        return tuple(cs) + tuple(accs)

    def diag_and_previous():
        cs, accs = steps([(diag_start, diag_mask), (pl.multiple_of((i - 1) * bq, bq), None)], c_zero, acc_zero)
        return tuple(cs) + tuple(accs)

    state0 = lax.cond(i == 0, diag_only, diag_and_previous)

    def cond(state):
        c_min = functools.reduce(jnp.minimum, state[1:1 + nh])
        return jnp.logical_and(state[0] >= 0, jnp.min(c_min) < SB_SKIP_THRESH)

    def body(state):
        j = state[0]
        cs, accs = steps([(pl.multiple_of(j * bq, bq), None)], state[1:1 + nh], state[1 + nh:])
        return (j - 1,) + tuple(cs) + tuple(accs)

    final = lax.while_loop(cond, body, (i - 2,) + tuple(state0))
    accs = final[1 + nh:]
    for pr in range(SB_PAIRS):
        o_ref[:, pr * LANES:(pr + 1) * LANES] = jnp.where(
            lane < SB_HEAD_DIM, accs[2 * pr], accs[2 * pr + 1]).astype(BF16)


def _sb_call(q, k, v, batch, seq):
    N = q.shape[0]
    nq = seq // SB_BQ
    width = SB_PAIRS * LANES
    return pl.pallas_call(
        _sb_kernel,
        grid=(batch, SB_WIDTH // width, nq),
        in_specs=[
            pl.BlockSpec((SB_BQ, width), lambda b, p, i: (b * nq + i, p)),
            pl.BlockSpec((seq, width), lambda b, p, i: (b, p)),
            pl.BlockSpec((seq, width), lambda b, p, i: (b, p)),
        ],
        out_specs=pl.BlockSpec((SB_BQ, width), lambda b, p, i: (b * nq + i, p)),
        out_shape=jax.ShapeDtypeStruct((N, SB_WIDTH), BF16),
        compiler_params=_params(("parallel", "parallel", "arbitrary")),
        name="sb_attn",
    )(q, k, v)


def _gla_kernel(gq_ref, gk_ref, gv_ref, gr_ref, lg_ref, gain_ref, o_ref, st_ref):
    C, SUB, H = GLA_CHUNK, GLA_SUB, GLA_HEADS
    nt = (((1,), (1,)), ((), ()))

    @pl.when(pl.program_id(1) == 0)
    def _():
        st_ref[...] = jnp.zeros_like(st_ref)

    rows = lax.broadcasted_iota(jnp.int32, (SUB, SUB), 0)
    cols = lax.broadcasted_iota(jnp.int32, (SUB, SUB), 1)
    same_chunk = (rows // C) == (cols // C)
    causal = same_chunk & (rows >= cols)
    low = jnp.where(causal, 1.0, 0.0).astype(BF16)
    lane = lax.broadcasted_iota(jnp.int32, (1, GLA_K_WIDTH), 1)
    in_head = [(lane >= h * GLA_DK) & (lane < (h + 1) * GLA_DK) for h in range(H)]
    gain = gain_ref[...]

    st = st_ref[...]
    for sb in range(GLA_TS // SUB):
        r0 = sb * SUB
        hi, lo = _split_hi_lo(lg_ref[r0:r0 + SUB, :])
        b = jnp.dot(low, hi, preferred_element_type=F32) + jnp.dot(low, lo, preferred_element_type=F32)
        b_last = jnp.concatenate(
            [jnp.broadcast_to(b[c0 + C - 1:c0 + C, :], (C, GLA_K_WIDTH)) for c0 in range(0, SUB, C)], axis=0)
        gq = gq_ref[r0:r0 + SUB, :].astype(F32)
        gk = gk_ref[r0:r0 + SUB, :].astype(F32)
        q_dec = gq * jnp.exp(b) * (GLA_DK ** -0.5)
        k_inv = (gk * jnp.exp(-b)).astype(BF16)
        k_dec = (gk * jnp.exp(b_last - b)).astype(BF16)
        decay = jnp.exp(b_last)
        qh = [jnp.where(in_head[h], q_dec, 0.0).astype(BF16) for h in range(H)]
        kh = [jnp.where(in_head[h], k_dec, jnp.zeros_like(k_dec)) for h in range(H)]

        intra = []
        for h in range(H):
            s = lax.dot_general(qh[h], k_inv, nt, preferred_element_type=F32)
            s = jnp.where(causal, s, 0.0).astype(BF16)
            vh = gv_ref[r0:r0 + SUB, h * GLA_DV:(h + 1) * GLA_DV]
            intra.append(jnp.dot(s, vh, preferred_element_type=F32))

        kvs = []
        for c0 in range(0, SUB, C):
            kv = None
            for h in range(H):
                v_t = gv_ref[r0 + c0:r0 + c0 + C, h * GLA_DV:(h + 1) * GLA_DV].astype(F32).T.astype(BF16)
                part = jnp.dot(v_t, kh[h][c0:c0 + C], preferred_element_type=F32)
                kv = part if kv is None else kv + part
            kvs.append(kv)
        states = []
        for c, c0 in enumerate(range(0, SUB, C)):
            states.append(st.astype(BF16))
            st = st * decay[c0:c0 + 1, :] + kvs[c]
        inter = [[] for _ in range(H)]
        for c, c0 in enumerate(range(0, SUB, C)):
            for h in range(H):
                inter[h].append(lax.dot_general(qh[h][c0:c0 + C], states[c], nt, preferred_element_type=F32))

        for h in range(H):
            o = intra[h] + jnp.concatenate(inter[h], axis=0)
            y = o * lax.rsqrt(jnp.mean(o * o, axis=-1, keepdims=True) + EPS) * gain
            g = gr_ref[r0:r0 + SUB, h * GLA_DV:(h + 1) * GLA_DV].astype(F32)
            y = y * (g * (1.0 / (1.0 + jnp.exp(-g))))
            o_ref[r0:r0 + SUB, h * GLA_DV:(h + 1) * GLA_DV] = y.astype(BF16)
    st_ref[...] = st


def _gla_call(gq, gk, gv, gr, lg, gain, batch, seq):
    N = gq.shape[0]
    nt = seq // GLA_TS
    row = lambda b, t: (b * nt + t, 0)
    return pl.pallas_call(
        _gla_kernel,
        grid=(batch, nt),
        in_specs=[
            pl.BlockSpec((GLA_TS, GLA_K_WIDTH), row),
            pl.BlockSpec((GLA_TS, GLA_K_WIDTH), row),
            pl.BlockSpec((GLA_TS, GLA_V_WIDTH), row),
            pl.BlockSpec((GLA_TS, GLA_V_WIDTH), row),
            pl.BlockSpec((GLA_TS, GLA_K_WIDTH), row),
            pl.BlockSpec((1, GLA_DV), lambda b, t: (0, 0)),
        ],
        out_specs=pl.BlockSpec((GLA_TS, GLA_V_WIDTH), row),
        out_shape=jax.ShapeDtypeStruct((N, GLA_V_WIDTH), BF16),
        scratch_shapes=[pltpu.VMEM((GLA_DV, GLA_K_WIDTH), F32)],
        compiler_params=_params(("parallel", "arbitrary")),
        name="gla",
    )(gq, gk, gv, gr, lg, gain)


def _out_kernel(a_ref, b_ref, x_ref, mod_ref, g_ref, wa_ref, wb_ref, wrh_ref, wrl_ref, br_ref,
                x1_ref, h2a_ref, h2b_ref, route_ref, route_t_ref, cnt_ref, run_ref):
    @pl.when(pl.program_id(0) == 0)
    def _():
        run_ref[...] = jnp.zeros_like(run_ref)

    gate_m = mod_ref[2:3, :]
    shift_f = mod_ref[3:4, :]
    scale_f = mod_ref[4:5, :]
    y = jnp.dot(a_ref[...], wa_ref[...], preferred_element_type=F32)
    y = y + jnp.dot(b_ref[...], wb_ref[...], preferred_element_type=F32)
    x1 = x_ref[...] + gate_m * y
    x1_ref[...] = x1
    n = x1 * lax.rsqrt(jnp.mean(x1 * x1, axis=-1, keepdims=True) + EPS) * g_ref[...]
    h2 = n * (1.0 + scale_f) + shift_f
    h_hi, h_lo = _split_hi_lo(h2)
    h2a_ref[...], h2b_ref[...] = _pack_rows(h2)

    nt = (((1,), (1,)), ((), ()))
    wr_hi = wrh_ref[...]
    lt = lax.dot_general(wr_hi, h_hi, nt, preferred_element_type=F32)
    lt = lt + lax.dot_general(wr_hi, h_lo, nt, preferred_element_type=F32)
    lt = lt + lax.dot_general(wrl_ref[...], h_hi, nt, preferred_element_type=F32)
    lt = lt + br_ref[...]
    tm = lt.shape[1]
    row_of = lambda r: lt[r:r + 1, :]

    def first_argmax(vals):
        best = functools.reduce(jnp.maximum, vals)
        idx = jnp.full((1, tm), len(vals) - 1, jnp.int32)
        for j in reversed(range(len(vals) - 1)):
            idx = jnp.where(vals[j] == best, j, idx)
        return best, idx

    glog = [row_of(N_EXPERTS + j) for j in range(N_GROUPS)]
    gmax, gidx = first_argmax(glog)
    grp_w = 1.0 / functools.reduce(jnp.add, [jnp.exp(v - gmax) for v in glog])

    elog = []
    for e in range(EXPERTS_PER_GROUP):
        v = row_of((N_GROUPS - 1) * EXPERTS_PER_GROUP + e)
        for j in reversed(range(N_GROUPS - 1)):
            v = jnp.where(gidx == j, row_of(j * EXPERTS_PER_GROUP + e), v)
        elog.append(v)
    m1, i1 = first_argmax(elog)
    neg = jnp.float32(-jnp.inf)
    m2, i2 = first_argmax([jnp.where(i1 == e, neg, v) for e, v in enumerate(elog)])
    e21 = jnp.exp(m2 - m1)
    p1 = 1.0 / (1.0 + e21)
    p2 = e21 * p1
    ea = gidx * EXPERTS_PER_GROUP + i1
    eb = gidx * EXPERTS_PER_GROUP + i2

    rows = lax.broadcasted_iota(jnp.int32, (N_EXPERTS, tm), 0)
    sel_a = rows == ea
    sel_b = rows == eb
    onehot = jnp.where(sel_a | sel_b, 1.0, 0.0)
    rr = lax.broadcasted_iota(jnp.int32, (tm, tm), 0)
    cc = lax.broadcasted_iota(jnp.int32, (tm, tm), 1)
    earlier = jnp.where(rr < cc, 1.0, 0.0).astype(BF16)
    seen = jnp.dot(onehot.astype(BF16), earlier, preferred_element_type=F32) + run_ref[:, 0:1]
    rank_a = jnp.sum(jnp.where(sel_a, seen, 0.0), axis=0, keepdims=True)
    rank_b = jnp.sum(jnp.where(sel_b, seen, 0.0), axis=0, keepdims=True)
    run = run_ref[...] + jnp.sum(onehot, axis=1, keepdims=True)
    run_ref[...] = run
    cnt_ref[...] = run

    fields = (ea.astype(F32), eb.astype(F32), p1 * grp_w, p2 * grp_w, rank_a, rank_b)
    field_row = lax.broadcasted_iota(jnp.int32, (LANES, tm), 0)
    route_t = jnp.zeros((LANES, tm), F32)
    for idx, val in enumerate(fields):
        route_t = jnp.where(field_row == idx, val, route_t)
    route_t_ref[...] = route_t[:SUBLANES, :]
    route_ref[...] = route_t.T


def _out_call(oa, ob, x, mod, g, w_out, wr_hi, wr_lo, br, seq, layer):
    N, D = x.shape
    tm = TM_OUT
    per_b = seq // tm
    row = lambda i: (i, 0)
    const = lambda i: (0, 0)
    return pl.pallas_call(
        _out_kernel,
        grid=(N // tm,),
        in_specs=[
            pl.BlockSpec((tm, SB_WIDTH), row),
            pl.BlockSpec((tm, GLA_V_WIDTH), row),
            pl.BlockSpec((tm, D), row),
            pl.BlockSpec((None, N_MOD, D), lambda i: (i // per_b, 0, 0)),
            pl.BlockSpec((1, D), const),
            pl.BlockSpec((None, SB_WIDTH, D), lambda i: (layer, 0, 0)),
            pl.BlockSpec((None, GLA_V_WIDTH, D), lambda i: (layer, 1, 0)),
            pl.BlockSpec((None, ROUTER_ROWS, D), lambda i: (layer, 0, 0)),
            pl.BlockSpec((None, ROUTER_ROWS, D), lambda i: (layer, 0, 0)),
            pl.BlockSpec((None, ROUTER_ROWS, 1), lambda i: (layer, 0, 0)),
        ],
        out_specs=[pl.BlockSpec((tm, D), row), pl.BlockSpec((tm, D_PACK), row), pl.BlockSpec((tm, D_PACK), row),
                   pl.BlockSpec((tm, LANES), row), pl.BlockSpec((SUBLANES, tm), lambda i: (0, i)),
                   pl.BlockSpec((N_EXPERTS, LANES), const)],
        out_shape=[jax.ShapeDtypeStruct((N, D), F32), jax.ShapeDtypeStruct((N, D_PACK), U32),
                   jax.ShapeDtypeStruct((N, D_PACK), U32), jax.ShapeDtypeStruct((N, LANES), F32),
                   jax.ShapeDtypeStruct((SUBLANES, N), F32), jax.ShapeDtypeStruct((N_EXPERTS, LANES), F32)],
        scratch_shapes=[pltpu.VMEM((N_EXPERTS, LANES), F32)],
        compiler_params=_params(("arbitrary",)),
        name="out_proj_router",
    )(oa, ob, x, mod, g, w_out, w_out, wr_hi, wr_lo, br)


def _sc_mesh():
    return plsc.VectorSubcoreMesh(core_axis_name="c", subcore_axis_name="s")


def _sc_dispatch(halves, pos, n_rows):
    N, dh = halves[0].shape
    w = SC_WINDOW
    nblk = N // w
    out_type = [jax.ShapeDtypeStruct((n_rows, dh), h.dtype) for h in halves]

    @pl.kernel(out_type=out_type, mesh=_sc_mesh(), scratch_types=[], name="moe_dispatch")
    def run(xa_hbm, xb_hbm, i_hbm, oa_hbm, ob_hbm):
        for x_hbm, o_hbm in ((xa_hbm, oa_hbm), (xb_hbm, ob_hbm)):
            def body(x_vmem, i_vmem, o_hbm=o_hbm):
                pltpu.sync_copy(x_vmem, o_hbm.at[i_vmem.at[0]])

            pltpu.emit_pipeline(
                body,
                grid=(2 * nblk,),
                in_specs=[pl.BlockSpec((w, dh), lambda j: (j % nblk, 0)),
                          pl.BlockSpec((1, w), lambda j: (0, j))],
                out_specs=[],
                core_axis_name=("c", "s"),
                dimension_semantics=(pltpu.PARALLEL,),
            )(x_hbm, i_hbm)

    return run(halves[0], halves[1], pos)


def _sc_collect(halves, pos):
    dh = halves[0].shape[1]
    n_idx = pos.shape[1]
    w = SC_WINDOW
    out_type = [jax.ShapeDtypeStruct((n_idx, dh), h.dtype) for h in halves]

    @pl.kernel(out_type=out_type, mesh=_sc_mesh(), scratch_types=[], name="moe_collect")
    def run(ya_hbm, yb_hbm, i_hbm, oa_hbm, ob_hbm):
        for y_hbm, o_hbm in ((ya_hbm, oa_hbm), (yb_hbm, ob_hbm)):
            def body(i_vmem, o_vmem, y_hbm=y_hbm):
                pltpu.sync_copy(y_hbm.at[i_vmem.at[0]], o_vmem)

            pltpu.emit_pipeline(
                body,
                grid=(n_idx // w,),
                in_specs=[pl.BlockSpec((1, w), lambda j: (0, j))],
                out_specs=[pl.BlockSpec((w, dh), lambda j: (j, 0))],
                core_axis_name=("c", "s"),
                dimension_semantics=(pltpu.PARALLEL,),
            )(i_hbm, o_hbm)

    return run(halves[0], halves[1], pos)


def _experts_kernel(te_ref, na_ref, xa_ref, xb_ref, wg_ref, wu_ref, wd_ref, oa_ref, ob_ref):
    del te_ref

    @pl.when(pl.program_id(0) < na_ref[0])
    def _():
        x = _unpack_rows(xa_ref[...], xb_ref[...]).astype(BF16)
        a = jnp.dot(x, wg_ref[...].astype(BF16), preferred_element_type=F32)
        u = jnp.dot(x, wu_ref[...].astype(BF16), preferred_element_type=F32)
        act = (a * (1.0 / (1.0 + jnp.exp(-a)))) * u
        y = jnp.dot(act.astype(BF16), wd_ref[...].astype(BF16), preferred_element_type=F32)
        oa_ref[...], ob_ref[...] = _pack_rows(y)


def _experts_call(tile_expert, n_active, xs, wg, wu, wd):
    R, dh = xs[0].shape
    tm = TM_MOE
    rows = lambda i, te, na: (jnp.minimum(i, na[0] - 1), 0)
    grid_spec = pltpu.PrefetchScalarGridSpec(
        num_scalar_prefetch=2,
        grid=(R // tm,),
        in_specs=[
            pl.BlockSpec((tm, dh), rows),
            pl.BlockSpec((tm, dh), rows),
            pl.BlockSpec((None, D_MODEL, D_EXPERT), lambda i, te, na: (te[i], 0, 0)),
            pl.BlockSpec((None, D_MODEL, D_EXPERT), lambda i, te, na: (te[i], 0, 0)),
            pl.BlockSpec((None, D_EXPERT, D_MODEL), lambda i, te, na: (te[i], 0, 0)),
        ],
        out_specs=[pl.BlockSpec((tm, dh), rows), pl.BlockSpec((tm, dh), rows)],
    )
    return pl.pallas_call(
        _experts_kernel,
        grid_spec=grid_spec,
        out_shape=[jax.ShapeDtypeStruct((R, dh), U32)] * 2,
        compiler_params=_params(("arbitrary",)),
        name="moe_experts",
    )(tile_expert, n_active, xs[0], xs[1], wg, wu, wd)


def _combine_kernel(*refs):
    refs[-1][...] = _moe_residual(*refs[:-1])


def _combine_call(x1, ys, route, mod, seq):
    N, D = x1.shape
    tm = TM_OUT
    per_b = seq // tm
    nblk = N // tm
    dh = ys[0].shape[1]
    row = lambda i: (i, 0)
    second = lambda i: (i + nblk, 0)
    return pl.pallas_call(
        _combine_kernel,
        grid=(nblk,),
        in_specs=[
            pl.BlockSpec((tm, D), row),
            pl.BlockSpec((tm, dh), row),
            pl.BlockSpec((tm, dh), row),
            pl.BlockSpec((tm, dh), second),
            pl.BlockSpec((tm, dh), second),
            pl.BlockSpec((tm, LANES), row),
            pl.BlockSpec((None, N_MOD, D), lambda i: (i // per_b, 0, 0)),
        ],
        out_specs=pl.BlockSpec((tm, D), row),
        out_shape=jax.ShapeDtypeStruct((N, D), F32),
        compiler_params=_params(("parallel",)),
        name="moe_combine",
    )(x1, ys[0], ys[1], ys[0], ys[1], route, mod)


def _routing_tables(route_t, counts, n_rows):
    tm = TM_MOE
    cnt = counts[:, 0].astype(jnp.int32)
    padded = ((cnt + tm - 1) // tm) * tm
    ends = jnp.cumsum(padded)
    base = ends - padded
    expert_ids = jnp.arange(N_EXPERTS, dtype=jnp.int32)[:, None]

    def rows_of(expert, rank):
        first = jnp.sum(jnp.where(expert_ids == expert[None, :], base[:, None], 0), axis=0)
        return first + rank

    rt = route_t.astype(jnp.int32)
    pos = jnp.concatenate([rows_of(rt[0], rt[4]), rows_of(rt[1], rt[5])]).reshape(1, -1)
    n_active = (ends[-1] // tm).reshape(1)
    tile_start = jnp.arange(n_rows // tm, dtype=jnp.int32) * tm
    tile_start = jnp.minimum(tile_start, ends[-1] - tm)
    tile_expert = jnp.sum(tile_start[:, None] >= ends[None, :], axis=1).astype(jnp.int32)
    return pos, tile_expert, n_active


def kernel(x, c, w_ada, b_ada, norm_mix, norm_ffn, w_in, w_gk2, b_gk, q_gain, k_gain, gla_gain, w_out,
           w_router_grp, b_router_grp, w_router_exp, b_router_exp, w_gate, w_up, w_down):
    B, S, D = x.shape
    L = w_ada.shape[0]
    N = B * S

    mod_all = _ada_call(c, w_ada, b_ada).reshape(L, B, N_MOD, D)

    w_in_p = jnp.pad(w_in, ((0, 0), (0, 0), (0, D_IN_PAD - w_in.shape[-1]))).astype(BF16)
    w_gk2_p = jnp.pad(w_gk2, ((0, 0), (0, LANES - GLA_RANK), (0, 0))).astype(BF16)
    qg = jnp.tile(q_gain, (1, SB_HEADS)).reshape(L, 1, SB_WIDTH)
    kg = jnp.tile(k_gain, (1, SB_HEADS)).reshape(L, 1, SB_WIDTH)
    w_out_b = w_out.astype(BF16)
    w_r = jnp.swapaxes(jnp.concatenate([w_router_exp, w_router_grp], axis=-1), 1, 2)
    w_r = jnp.pad(w_r, ((0, 0), (0, ROUTER_ROWS - w_r.shape[1]), (0, 0)))
    w_r_hi = w_r.astype(BF16)
    w_r_lo = (w_r - w_r_hi.astype(F32)).astype(BF16)
    b_r = jnp.concatenate([b_router_exp, b_router_grp], axis=-1)
    b_r = jnp.pad(b_r, ((0, 0), (0, ROUTER_ROWS - b_r.shape[-1]))).reshape(L, ROUTER_ROWS, 1)
    wg = w_gate.reshape(L * N_EXPERTS, D, D_EXPERT)
    wu = w_up.reshape(L * N_EXPERTS, D, D_EXPERT)
    wd = w_down.reshape(L * N_EXPERTS, D_EXPERT, D)

    n_rows = 2 * N + N_EXPERTS * TM_MOE
    xf = x.reshape(N, D)
    moe = None
    for l in range(L):
        mod = mod_all[l]
        outs = _proj_call(
            xf, moe, mod, norm_mix[l].reshape(1, D), w_in_p, qg[l], kg[l], w_gk2_p[l],
            b_gk[l].reshape(1, GLA_K_WIDTH), S, l)
        if moe is not None:
            xf, outs = outs[0], outs[1:]
        q, k, v, gq, gk, gv, gr, lg = outs
        out_a = _sb_call(q, k, v, B, S)
        out_b = _gla_call(gq, gk, gv, gr, lg, gla_gain[l].reshape(1, GLA_DV), B, S)
        x1, h2a, h2b, route, route_t, counts = _out_call(out_a, out_b, xf, mod, norm_ffn[l].reshape(1, D),
                                          w_out_b, w_r_hi, w_r_lo, b_r, S, l)
        pos, tile_expert, n_active = _routing_tables(route_t, counts, n_rows)
        xs = _sc_dispatch((h2a, h2b), pos, n_rows)
        ys = _experts_call(tile_expert + l * N_EXPERTS, n_active, xs, wg, wu, wd)
        yt = _sc_collect(ys, pos)
        xf, moe = x1, (yt, route, mod)
    return _combine_call(xf, moe[0], moe[1], moe[2], S).reshape(B, S, D)
```

```python
import functools

import jax
import jax.numpy as jnp
from jax import lax
from jax.experimental import pallas as pl
from jax.experimental.pallas import tpu as pltpu
from jax.experimental.pallas import tpu_sc as plsc

F32 = jnp.float32
BF16 = jnp.bfloat16

D_MODEL = 1024
SB_HEADS = 8
SB_HEAD_DIM = 64
SB_WIDTH = 512
GLA_HEADS = 4
GLA_DK = 64
GLA_DV = 128
GLA_K_WIDTH = 256
GLA_V_WIDTH = 512
GLA_RANK = 16
GLA_GATE_NORM = 16.0
GLA_CHUNK = 64
N_GROUPS = 4
EXPERTS_PER_GROUP = 8
N_EXPERTS = N_GROUPS * EXPERTS_PER_GROUP
D_EXPERT = 256
N_MOD = 6
ROUTER_ROWS = 48
EPS = 1e-6

LANES = 128
SUBLANES = 8
VMEM_LIMIT = 56 * 1024 * 1024

_C_Q, _C_K, _C_V = 0, 512, 1024
_C_GQ, _C_GK, _C_GV, _C_GR, _C_LR = 1536, 1792, 2048, 2560, 3072
D_IN_PAD = 3200

TM_PROJ = 512
TM_OUT = 512
TM_MOE = 512
N_STREAMS = 2
SC_WINDOW = 128
D_PACK = D_MODEL // 4
U32 = jnp.uint32
SB_BQ = 256
SB_PAIRS = 4
SB_GROUP = 2
GLA_TS = 512
GLA_SUB = 256
LOG2E = 1.4426950408889634
SB_SKIP_THRESH = 160.0


def _params(sem):
    return pltpu.CompilerParams(dimension_semantics=sem, vmem_limit_bytes=VMEM_LIMIT)


def _split_hi_lo(x):
    hi = x.astype(BF16)
    lo = (x - hi.astype(F32)).astype(BF16)
    return hi, lo


def _pack_rows(x):
    r = pltpu.bitcast(x.astype(BF16).astype(F32), U32)
    out = []
    for half in range(2):
        c0 = half * 2 * D_PACK
        out.append(r[:, c0:c0 + D_PACK] | (r[:, c0 + D_PACK:c0 + 2 * D_PACK] >> 16))
    return out


def _unpack_rows(pa, pb):
    parts = []
    for p in (pa, pb):
        parts.append(pltpu.bitcast(p & U32(0xFFFF0000), F32))
        parts.append(pltpu.bitcast(p << 16, F32))
    return jnp.concatenate(parts, axis=1)


def _softplus2(z2):
    neg_abs = pltpu.bitcast(pltpu.bitcast(z2, jnp.uint32) | jnp.uint32(0x80000000), F32)
    return jnp.maximum(z2, 0.0) + jnp.log2(1.0 + jnp.exp2(neg_abs))


def _ada_kernel(c_ref, w_ref, b_ref, o_ref):
    c = c_ref[...]
    ca = (c * (1.0 / (1.0 + jnp.exp(-c)))).astype(BF16)
    o_ref[...] = jnp.dot(ca, w_ref[...].astype(BF16), preferred_element_type=F32) + b_ref[...]


def _ada_call(c, w_ada, b_ada):
    L, D, _ = w_ada.shape
    B = c.shape[0]
    return pl.pallas_call(
        _ada_kernel,
        grid=(L, N_MOD),
        in_specs=[
            pl.BlockSpec((B, D), lambda l, j: (0, 0)),
            pl.BlockSpec((None, D, D), lambda l, j: (l, 0, j)),
            pl.BlockSpec((None, 1, D), lambda l, j: (l, 0, j)),
        ],
        out_specs=pl.BlockSpec((None, B, D), lambda l, j: (l, 0, j)),
        out_shape=jax.ShapeDtypeStruct((L, B, N_MOD * D), F32),
        compiler_params=_params(("parallel", "parallel")),
        name="ada_mod",
    )(c, w_ada, b_ada.reshape(L, 1, N_MOD * D))


def _moe_residual(x1_ref, y1a_ref, y1b_ref, y2a_ref, y2b_ref, route_ref, mod_ref):
    route = route_ref[...]
    lane = lax.broadcasted_iota(jnp.int32, route.shape, 1)
    w1 = jnp.sum(jnp.where(lane == 2, route, 0.0), axis=-1, keepdims=True)
    w2 = jnp.sum(jnp.where(lane == 3, route, 0.0), axis=-1, keepdims=True)
    y1 = _unpack_rows(y1a_ref[...], y1b_ref[...])
    y2 = _unpack_rows(y2a_ref[...], y2b_ref[...])
    return x1_ref[...] + mod_ref[5:6, :] * (w1 * y1 + w2 * y2)


def _proj_kernel(after_moe, *refs):
    if after_moe:
        moe_refs, refs = refs[:7], refs[7:]
    else:
        x_ref, refs = refs[0], refs[1:]
    mod_ref, g_ref, w_ref, qg_ref, kg_ref, wgk_ref, bgk_ref = refs[:7]
    outs = refs[7:]
    if after_moe:
        x = _moe_residual(*moe_refs)
        outs[0][...] = x
        outs = outs[1:]
    else:
        x = x_ref[...]
    q_ref, k_ref, v_ref, gq_ref, gk_ref, gv_ref, gr_ref, lg_ref = outs
    shift = mod_ref[0:1, :]
    scale = mod_ref[1:2, :]
    y = x * lax.rsqrt(jnp.mean(x * x, axis=-1, keepdims=True) + EPS) * g_ref[...]
    h = (y * (1.0 + scale) + shift).astype(BF16)

    def proj(c0, width):
        return jnp.dot(h, w_ref[:, c0:c0 + width], preferred_element_type=F32)

    r = lax.broadcasted_iota(jnp.int32, (SB_WIDTH, SB_WIDTH), 0) // SB_HEAD_DIM
    c = lax.broadcasted_iota(jnp.int32, (SB_WIDTH, SB_WIDTH), 1) // SB_HEAD_DIM
    avg = jnp.where(r == c, 1.0 / SB_HEAD_DIM, 0.0).astype(BF16)

    def head_norm(t, gain):
        ms = jnp.dot((t * t).astype(BF16), avg, preferred_element_type=F32)
        return t * lax.rsqrt(ms + EPS) * gain

    q = head_norm(proj(_C_Q, SB_WIDTH), qg_ref[...])
    q_ref[...] = (q * (SB_HEAD_DIM ** -0.5 * LOG2E)).astype(BF16)
    k_ref[...] = head_norm(proj(_C_K, SB_WIDTH), kg_ref[...]).astype(BF16)
    v_ref[...] = proj(_C_V, SB_WIDTH).astype(BF16)
    gq_ref[...] = proj(_C_GQ, GLA_K_WIDTH).astype(BF16)
    gk_ref[...] = proj(_C_GK, GLA_K_WIDTH).astype(BF16)
    gv_ref[...] = proj(_C_GV, GLA_V_WIDTH).astype(BF16)
    gr_ref[...] = proj(_C_GR, GLA_V_WIDTH).astype(BF16)
    lr = proj(_C_LR, LANES).astype(BF16)
    pre = jnp.dot(lr, wgk_ref[...], preferred_element_type=F32) + bgk_ref[...]
    lg_ref[...] = (jnp.minimum(pre, 0.0) - jnp.log(1.0 + jnp.exp(-jnp.abs(pre)))) * (1.0 / GLA_GATE_NORM)


def _proj_call(x, moe, mod, g, w_in, q_gain, k_gain, w_gk2, b_gk, seq, layer):
    N, D = x.shape
    tm = TM_PROJ
    per_b = seq // tm
    nblk = N // tm
    row = lambda i: (i, 0)
    const = lambda i: (0, 0)
    mod_spec = pl.BlockSpec((None, N_MOD, D), lambda i: (i // per_b, 0, 0))
    widths = [(SB_WIDTH, BF16)] * 3 + [(GLA_K_WIDTH, BF16)] * 2 + [(GLA_V_WIDTH, BF16)] * 2 + [(GLA_K_WIDTH, F32)]
    x_args, x_specs = [x], [pl.BlockSpec((tm, D), row)]
    if moe is not None:
        ys, route, mod_prev = moe
        dh = ys[0].shape[1]
        second = lambda i: (i + nblk, 0)
        x_args += [ys[0], ys[1], ys[0], ys[1], route, mod_prev]
        x_specs += [pl.BlockSpec((tm, dh), row), pl.BlockSpec((tm, dh), row), pl.BlockSpec((tm, dh), second),
                    pl.BlockSpec((tm, dh), second), pl.BlockSpec((tm, LANES), row), mod_spec]
        widths = [(D, F32)] + widths
    return pl.pallas_call(
        functools.partial(_proj_kernel, moe is not None),
        grid=(nblk,),
        in_specs=x_specs + [
            mod_spec,
            pl.BlockSpec((1, D), const),
            pl.BlockSpec((None, D, D_IN_PAD), lambda i: (layer, 0, 0)),
            pl.BlockSpec((1, SB_WIDTH), const),
            pl.BlockSpec((1, SB_WIDTH), const),
            pl.BlockSpec((LANES, GLA_K_WIDTH), const),
            pl.BlockSpec((1, GLA_K_WIDTH), const),
        ],
        out_specs=[pl.BlockSpec((tm, w), row) for w, _ in widths],
        out_shape=[jax.ShapeDtypeStruct((N, w), dt) for w, dt in widths],
        compiler_params=_params(("parallel",)),
        name="in_proj",
    )(*x_args, mod, g, w_in, q_gain, k_gain, w_gk2, b_gk)


def _sb_kernel(q_ref, k_ref, v_ref, o_ref):
    i = pl.program_id(2)
    bq = SB_BQ
    lane = lax.broadcasted_iota(jnp.int32, (1, LANES), 1)
    qms = []
    for pr in range(SB_PAIRS):
        q = q_ref[:, pr * LANES:(pr + 1) * LANES]
        for hh in range(2):
            in_head = (lane >= hh * SB_HEAD_DIM) & (lane < (hh + 1) * SB_HEAD_DIM)
            qms.append(jnp.where(in_head, q, jnp.zeros_like(q)))
    nh = len(qms)

    t_loc = lax.broadcasted_iota(jnp.int32, (bq, bq), 0)
    s_loc = lax.broadcasted_iota(jnp.int32, (bq, bq), 1)
    diag_mask = s_loc < t_loc
    later = jnp.where(t_loc > s_loc, 1.0, 0.0).astype(BF16)

    def steps(starts_and_masks, cs, accs):
        cs, accs = list(cs), list(accs)
        for h0 in range(0, nh, SB_GROUP):
            stage_group(starts_and_masks, range(h0, h0 + SB_GROUP), cs, accs)
        return cs, accs

    def stage_group(starts_and_masks, heads, cs, accs):
        pairs = [(s, hh) for s in range(len(starts_and_masks)) for hh in heads]
        block = lambda ref, s, hh: ref[pl.ds(starts_and_masks[s][0], bq), (hh // 2) * LANES:(hh // 2 + 1) * LANES]
        z = {(s, hh): lax.dot_general(qms[hh], block(k_ref, s, hh), (((1,), (1,)), ((), ())),
                                      preferred_element_type=F32) for s, hh in pairs}
        lf, logsig = {}, {}
        for s, hh in pairs:
            mask = starts_and_masks[s][1]
            sp = _softplus2(z[s, hh])
            logsig[s, hh] = z[s, hh] - sp
            lf[s, hh] = (sp if mask is None else jnp.where(mask, sp, 0.0)).astype(BF16)
        after = {p: jnp.dot(lf[p], later, preferred_element_type=F32) for p in pairs}
        w = {}
        for s, hh in pairs:
            mask = starts_and_masks[s][1]
            wt = jnp.exp2(logsig[s, hh] - after[s, hh] - cs[hh])
            w[s, hh] = (wt if mask is None else jnp.where(mask, wt, 0.0)).astype(BF16)
            cs[hh] = cs[hh] + after[s, hh][:, 0:1] + lf[s, hh][:, 0:1].astype(F32)
        for s, hh in pairs:
            accs[hh] = accs[hh] + jnp.dot(w[s, hh], block(v_ref, s, hh), preferred_element_type=F32)

    c_zero = [jnp.zeros((bq, 1), F32)] * nh
    acc_zero = [jnp.zeros((bq, LANES), F32)] * nh
    diag_start = pl.multiple_of(i * bq, bq)

    def diag_only():
        cs, accs = steps([(diag_start, diag_mask)], c_zero, acc_zero)
        return tuple(cs) + tuple(accs)

    def diag_and_previous():
        cs, accs = steps([(diag_start, diag_mask), (pl.multiple_of((i - 1) * bq, bq), None)], c_zero, acc_zero)
        return tuple(cs) + tuple(accs)

    state0 = lax.cond(i == 0, diag_only, diag_and_previous)

    def cond(state):
        c_min = functools.reduce(jnp.minimum, state[1:1 + nh])
        return jnp.logical_and(state[0] >= 0, jnp.min(c_min) < SB_SKIP_THRESH)

    def body(state):
        j = state[0]
        cs, accs = steps([(pl.multiple_of(j * bq, bq), None)], state[1:1 + nh], state[1 + nh:])
        return (j - 1,) + tuple(cs) + tuple(accs)

    final = lax.while_loop(cond, body, (i - 2,) + tuple(state0))
    accs = final[1 + nh:]
    for pr in range(SB_PAIRS):
        o_ref[:, pr * LANES:(pr + 1) * LANES] = jnp.where(
            lane < SB_HEAD_DIM, accs[2 * pr], accs[2 * pr + 1]).astype(BF16)


def _sb_call(q, k, v, batch, seq):
    N = q.shape[0]
    nq = seq // SB_BQ
    width = SB_PAIRS * LANES
    return pl.pallas_call(
        _sb_kernel,
        grid=(batch, SB_WIDTH // width, nq),
        in_specs=[
            pl.BlockSpec((SB_BQ, width), lambda b, p, i: (b * nq + i, p)),
            pl.BlockSpec((seq, width), lambda b, p, i: (b, p)),
            pl.BlockSpec((seq, width), lambda b, p, i: (b, p)),
        ],
        out_specs=pl.BlockSpec((SB_BQ, width), lambda b, p, i: (b * nq + i, p)),
        out_shape=jax.ShapeDtypeStruct((N, SB_WIDTH), BF16),
        compiler_params=_params(("parallel", "parallel", "arbitrary")),
        name="sb_attn",
    )(q, k, v)


def _gla_kernel(gq_ref, gk_ref, gv_ref, gr_ref, lg_ref, gain_ref, o_ref, st_ref):
    C, SUB, H = GLA_CHUNK, GLA_SUB, GLA_HEADS
    nt = (((1,), (1,)), ((), ()))

    @pl.when(pl.program_id(1) == 0)
    def _():
        st_ref[...] = jnp.zeros_like(st_ref)

    rows = lax.broadcasted_iota(jnp.int32, (SUB, SUB), 0)
    cols = lax.broadcasted_iota(jnp.int32, (SUB, SUB), 1)
    same_chunk = (rows // C) == (cols // C)
    causal = same_chunk & (rows >= cols)
    low = jnp.where(causal, 1.0, 0.0).astype(BF16)
    lane = lax.broadcasted_iota(jnp.int32, (1, GLA_K_WIDTH), 1)
    in_head = [(lane >= h * GLA_DK) & (lane < (h + 1) * GLA_DK) for h in range(H)]
    gain = gain_ref[...]

    st = st_ref[...]
    for sb in range(GLA_TS // SUB):
        r0 = sb * SUB
        hi, lo = _split_hi_lo(lg_ref[r0:r0 + SUB, :])
        b = jnp.dot(low, hi, preferred_element_type=F32) + jnp.dot(low, lo, preferred_element_type=F32)
        b_last = jnp.concatenate(
            [jnp.broadcast_to(b[c0 + C - 1:c0 + C, :], (C, GLA_K_WIDTH)) for c0 in range(0, SUB, C)], axis=0)
        gq = gq_ref[r0:r0 + SUB, :].astype(F32)
        gk = gk_ref[r0:r0 + SUB, :].astype(F32)
        q_dec = gq * jnp.exp(b) * (GLA_DK ** -0.5)
        k_inv = (gk * jnp.exp(-b)).astype(BF16)
        k_dec = (gk * jnp.exp(b_last - b)).astype(BF16)
        decay = jnp.exp(b_last)
        qh = [jnp.where(in_head[h], q_dec, 0.0).astype(BF16) for h in range(H)]
        kh = [jnp.where(in_head[h], k_dec, jnp.zeros_like(k_dec)) for h in range(H)]

        intra = []
        for h in range(H):
            s = lax.dot_general(qh[h], k_inv, nt, preferred_element_type=F32)
            s = jnp.where(causal, s, 0.0).astype(BF16)
            vh = gv_ref[r0:r0 + SUB, h * GLA_DV:(h + 1) * GLA_DV]
            intra.append(jnp.dot(s, vh, preferred_element_type=F32))

        kvs = []
        for c0 in range(0, SUB, C):
            kv = None
            for h in range(H):
                v_t = gv_ref[r0 + c0:r0 + c0 + C, h * GLA_DV:(h + 1) * GLA_DV].astype(F32).T.astype(BF16)
                part = jnp.dot(v_t, kh[h][c0:c0 + C], preferred_element_type=F32)
                kv = part if kv is None else kv + part
            kvs.append(kv)
        states = []
        for c, c0 in enumerate(range(0, SUB, C)):
            states.append(st.astype(BF16))
            st = st * decay[c0:c0 + 1, :] + kvs[c]
        inter = [[] for _ in range(H)]
        for c, c0 in enumerate(range(0, SUB, C)):
            for h in range(H):
                inter[h].append(lax.dot_general(qh[h][c0:c0 + C], states[c], nt, preferred_element_type=F32))

        for h in range(H):
            o = intra[h] + jnp.concatenate(inter[h], axis=0)
            y = o * lax.rsqrt(jnp.mean(o * o, axis=-1, keepdims=True) + EPS) * gain
            g = gr_ref[r0:r0 + SUB, h * GLA_DV:(h + 1) * GLA_DV].astype(F32)
            y = y * (g * (1.0 / (1.0 + jnp.exp(-g))))
            o_ref[r0:r0 + SUB, h * GLA_DV:(h + 1) * GLA_DV] = y.astype(BF16)
    st_ref[...] = st


def _gla_call(gq, gk, gv, gr, lg, gain, batch, seq):
    N = gq.shape[0]
    nt = seq // GLA_TS
    row = lambda b, t: (b * nt + t, 0)
    return pl.pallas_call(
        _gla_kernel,
        grid=(batch, nt),
        in_specs=[
            pl.BlockSpec((GLA_TS, GLA_K_WIDTH), row),
            pl.BlockSpec((GLA_TS, GLA_K_WIDTH), row),
            pl.BlockSpec((GLA_TS, GLA_V_WIDTH), row),
            pl.BlockSpec((GLA_TS, GLA_V_WIDTH), row),
            pl.BlockSpec((GLA_TS, GLA_K_WIDTH), row),
            pl.BlockSpec((1, GLA_DV), lambda b, t: (0, 0)),
        ],
        out_specs=pl.BlockSpec((GLA_TS, GLA_V_WIDTH), row),
        out_shape=jax.ShapeDtypeStruct((N, GLA_V_WIDTH), BF16),
        scratch_shapes=[pltpu.VMEM((GLA_DV, GLA_K_WIDTH), F32)],
        compiler_params=_params(("parallel", "arbitrary")),
        name="gla",
    )(gq, gk, gv, gr, lg, gain)


def _out_kernel(a_ref, b_ref, x_ref, mod_ref, g_ref, wa_ref, wb_ref, wrh_ref, wrl_ref, br_ref,
                x1_ref, h2a_ref, h2b_ref, route_ref, route_t_ref, cnt_ref, run_ref):
    @pl.when(pl.program_id(0) == 0)
    def _():
        run_ref[...] = jnp.zeros_like(run_ref)

    gate_m = mod_ref[2:3, :]
    shift_f = mod_ref[3:4, :]
    scale_f = mod_ref[4:5, :]
    y = jnp.dot(a_ref[...], wa_ref[...], preferred_element_type=F32)
    y = y + jnp.dot(b_ref[...], wb_ref[...], preferred_element_type=F32)
    x1 = x_ref[...] + gate_m * y
    x1_ref[...] = x1
    n = x1 * lax.rsqrt(jnp.mean(x1 * x1, axis=-1, keepdims=True) + EPS) * g_ref[...]
    h2 = n * (1.0 + scale_f) + shift_f
    h_hi, h_lo = _split_hi_lo(h2)
    h2a_ref[...], h2b_ref[...] = _pack_rows(h2)

    nt = (((1,), (1,)), ((), ()))
    wr_hi = wrh_ref[...]
    lt = lax.dot_general(wr_hi, h_hi, nt, preferred_element_type=F32)
    lt = lt + lax.dot_general(wr_hi, h_lo, nt, preferred_element_type=F32)
    lt = lt + lax.dot_general(wrl_ref[...], h_hi, nt, preferred_element_type=F32)
    lt = lt + br_ref[...]
    tm = lt.shape[1]
    row_of = lambda r: lt[r:r + 1, :]

    def first_argmax(vals):
        best = functools.reduce(jnp.maximum, vals)
        idx = jnp.full((1, tm), len(vals) - 1, jnp.int32)
        for j in reversed(range(len(vals) - 1)):
            idx = jnp.where(vals[j] == best, j, idx)
        return best, idx

    glog = [row_of(N_EXPERTS + j) for j in range(N_GROUPS)]
    gmax, gidx = first_argmax(glog)
    grp_w = 1.0 / functools.reduce(jnp.add, [jnp.exp(v - gmax) for v in glog])

    elog = []
    for e in range(EXPERTS_PER_GROUP):
        v = row_of((N_GROUPS - 1) * EXPERTS_PER_GROUP + e)
        for j in reversed(range(N_GROUPS - 1)):
            v = jnp.where(gidx == j, row_of(j * EXPERTS_PER_GROUP + e), v)
        elog.append(v)
    m1, i1 = first_argmax(elog)
    neg = jnp.float32(-jnp.inf)
    m2, i2 = first_argmax([jnp.where(i1 == e, neg, v) for e, v in enumerate(elog)])
    e21 = jnp.exp(m2 - m1)
    p1 = 1.0 / (1.0 + e21)
    p2 = e21 * p1
    ea = gidx * EXPERTS_PER_GROUP + i1
    eb = gidx * EXPERTS_PER_GROUP + i2

    rows = lax.broadcasted_iota(jnp.int32, (N_EXPERTS, tm), 0)
    sel_a = rows == ea
    sel_b = rows == eb
    onehot = jnp.where(sel_a | sel_b, 1.0, 0.0)
    rr = lax.broadcasted_iota(jnp.int32, (tm, tm), 0)
    cc = lax.broadcasted_iota(jnp.int32, (tm, tm), 1)
    earlier = jnp.where(rr < cc, 1.0, 0.0).astype(BF16)
    seen = jnp.dot(onehot.astype(BF16), earlier, preferred_element_type=F32) + run_ref[:, 0:1]
    rank_a = jnp.sum(jnp.where(sel_a, seen, 0.0), axis=0, keepdims=True)
    rank_b = jnp.sum(jnp.where(sel_b, seen, 0.0), axis=0, keepdims=True)
    run = run_ref[...] + jnp.sum(onehot, axis=1, keepdims=True)
    run_ref[...] = run
    cnt_ref[...] = run

    fields = (ea.astype(F32), eb.astype(F32), p1 * grp_w, p2 * grp_w, rank_a, rank_b)
    field_row = lax.broadcasted_iota(jnp.int32, (LANES, tm), 0)
    route_t = jnp.zeros((LANES, tm), F32)
    for idx, val in enumerate(fields):
        route_t = jnp.where(field_row == idx, val, route_t)
    route_t_ref[...] = route_t[:SUBLANES, :]
    route_ref[...] = route_t.T


def _out_call(oa, ob, x, mod, g, w_out, wr_hi, wr_lo, br, seq, layer):
    N, D = x.shape
    tm = TM_OUT
    per_b = seq // tm
    row = lambda i: (i, 0)
    const = lambda i: (0, 0)
    return pl.pallas_call(
        _out_kernel,
        grid=(N // tm,),
        in_specs=[
            pl.BlockSpec((tm, SB_WIDTH), row),
            pl.BlockSpec((tm, GLA_V_WIDTH), row),
            pl.BlockSpec((tm, D), row),
            pl.BlockSpec((None, N_MOD, D), lambda i: (i // per_b, 0, 0)),
            pl.BlockSpec((1, D), const),
            pl.BlockSpec((None, SB_WIDTH, D), lambda i: (layer, 0, 0)),
            pl.BlockSpec((None, GLA_V_WIDTH, D), lambda i: (layer, 1, 0)),
            pl.BlockSpec((None, ROUTER_ROWS, D), lambda i: (layer, 0, 0)),
            pl.BlockSpec((None, ROUTER_ROWS, D), lambda i: (layer, 0, 0)),
            pl.BlockSpec((None, ROUTER_ROWS, 1), lambda i: (layer, 0, 0)),
        ],
        out_specs=[pl.BlockSpec((tm, D), row), pl.BlockSpec((tm, D_PACK), row), pl.BlockSpec((tm, D_PACK), row),
                   pl.BlockSpec((tm, LANES), row), pl.BlockSpec((SUBLANES, tm), lambda i: (0, i)),
                   pl.BlockSpec((N_EXPERTS, LANES), const)],
        out_shape=[jax.ShapeDtypeStruct((N, D), F32), jax.ShapeDtypeStruct((N, D_PACK), U32),
                   jax.ShapeDtypeStruct((N, D_PACK), U32), jax.ShapeDtypeStruct((N, LANES), F32),
                   jax.ShapeDtypeStruct((SUBLANES, N), F32), jax.ShapeDtypeStruct((N_EXPERTS, LANES), F32)],
        scratch_shapes=[pltpu.VMEM((N_EXPERTS, LANES), F32)],
        compiler_params=_params(("arbitrary",)),
        name="out_proj_router",
    )(oa, ob, x, mod, g, w_out, w_out, wr_hi, wr_lo, br)


def _sc_mesh():
    return plsc.VectorSubcoreMesh(core_axis_name="c", subcore_axis_name="s")


def _sc_dispatch(halves, pos, n_rows):
    N, dh = halves[0].shape
    w = SC_WINDOW
    nblk = N // w
    out_type = [jax.ShapeDtypeStruct((n_rows, dh), h.dtype) for h in halves]

    @pl.kernel(out_type=out_type, mesh=_sc_mesh(), scratch_types=[], name="moe_dispatch")
    def run(xa_hbm, xb_hbm, i_hbm, oa_hbm, ob_hbm):
        for x_hbm, o_hbm in ((xa_hbm, oa_hbm), (xb_hbm, ob_hbm)):
            def body(x_vmem, i_vmem, o_hbm=o_hbm):
                pltpu.sync_copy(x_vmem, o_hbm.at[i_vmem.at[0]])

            pltpu.emit_pipeline(
                body,
                grid=(2 * nblk,),
                in_specs=[pl.BlockSpec((w, dh), lambda j: (j % nblk, 0)),
                          pl.BlockSpec((1, w), lambda j: (0, j))],
                out_specs=[],
                core_axis_name=("c", "s"),
                dimension_semantics=(pltpu.PARALLEL,),
            )(x_hbm, i_hbm)

    return run(halves[0], halves[1], pos)


def _sc_collect(halves, pos):
    dh = halves[0].shape[1]
    n_idx = pos.shape[1]
    w = SC_WINDOW
    out_type = [jax.ShapeDtypeStruct((n_idx, dh), h.dtype) for h in halves]

    @pl.kernel(out_type=out_type, mesh=_sc_mesh(), scratch_types=[], name="moe_collect")
    def run(ya_hbm, yb_hbm, i_hbm, oa_hbm, ob_hbm):
        for y_hbm, o_hbm in ((ya_hbm, oa_hbm), (yb_hbm, ob_hbm)):
            def body(i_vmem, o_vmem, y_hbm=y_hbm):
                pltpu.sync_copy(y_hbm.at[i_vmem.at[0]], o_vmem)

            pltpu.emit_pipeline(
                body,
                grid=(n_idx // w,),
                in_specs=[pl.BlockSpec((1, w), lambda j: (0, j))],
                out_specs=[pl.BlockSpec((w, dh), lambda j: (j, 0))],
                core_axis_name=("c", "s"),
                dimension_semantics=(pltpu.PARALLEL,),
            )(i_hbm, o_hbm)

    return run(halves[0], halves[1], pos)


def _experts_kernel(te_ref, na_ref, xa_ref, xb_ref, wg_ref, wu_ref, wd_ref, oa_ref, ob_ref):
    del te_ref

    @pl.when(pl.program_id(0) < na_ref[0])
    def _():
        x = _unpack_rows(xa_ref[...], xb_ref[...]).astype(BF16)
        a = jnp.dot(x, wg_ref[...].astype(BF16), preferred_element_type=F32)
        u = jnp.dot(x, wu_ref[...].astype(BF16), preferred_element_type=F32)
        act = (a * (1.0 / (1.0 + jnp.exp(-a)))) * u
        y = jnp.dot(act.astype(BF16), wd_ref[...].astype(BF16), preferred_element_type=F32)
        oa_ref[...], ob_ref[...] = _pack_rows(y)


def _experts_call(tile_expert, n_active, xs, wg, wu, wd):
    R, dh = xs[0].shape
    tm = TM_MOE
    rows = lambda i, te, na: (jnp.minimum(i, na[0] - 1), 0)
    grid_spec = pltpu.PrefetchScalarGridSpec(
        num_scalar_prefetch=2,
        grid=(R // tm,),
        in_specs=[
            pl.BlockSpec((tm, dh), rows),
            pl.BlockSpec((tm, dh), rows),
            pl.BlockSpec((None, D_MODEL, D_EXPERT), lambda i, te, na: (te[i], 0, 0)),
            pl.BlockSpec((None, D_MODEL, D_EXPERT), lambda i, te, na: (te[i], 0, 0)),
            pl.BlockSpec((None, D_EXPERT, D_MODEL), lambda i, te, na: (te[i], 0, 0)),
        ],
        out_specs=[pl.BlockSpec((tm, dh), rows), pl.BlockSpec((tm, dh), rows)],
    )
    return pl.pallas_call(
        _experts_kernel,
        grid_spec=grid_spec,
        out_shape=[jax.ShapeDtypeStruct((R, dh), U32)] * 2,
        compiler_params=_params(("arbitrary",)),
        name="moe_experts",
    )(tile_expert, n_active, xs[0], xs[1], wg, wu, wd)


def _combine_kernel(*refs):
    refs[-1][...] = _moe_residual(*refs[:-1])


def _combine_call(x1, ys, route, mod, seq):
    N, D = x1.shape
    tm = TM_OUT
    per_b = seq // tm
    nblk = N // tm
    dh = ys[0].shape[1]
    row = lambda i: (i, 0)
    second = lambda i: (i + nblk, 0)
    return pl.pallas_call(
        _combine_kernel,
        grid=(nblk,),
        in_specs=[
            pl.BlockSpec((tm, D), row),
            pl.BlockSpec((tm, dh), row),
            pl.BlockSpec((tm, dh), row),
            pl.BlockSpec((tm, dh), second),
            pl.BlockSpec((tm, dh), second),
            pl.BlockSpec((tm, LANES), row),
            pl.BlockSpec((None, N_MOD, D), lambda i: (i // per_b, 0, 0)),
        ],
        out_specs=pl.BlockSpec((tm, D), row),
        out_shape=jax.ShapeDtypeStruct((N, D), F32),
        compiler_params=_params(("parallel",)),
        name="moe_combine",
    )(x1, ys[0], ys[1], ys[0], ys[1], route, mod)


def _routing_tables(route_t, counts, n_rows):
    tm = TM_MOE
    cnt = counts[:, 0].astype(jnp.int32)
    padded = ((cnt + tm - 1) // tm) * tm
    ends = jnp.cumsum(padded)
    base = ends - padded
    expert_ids = jnp.arange(N_EXPERTS, dtype=jnp.int32)[:, None]

    def rows_of(expert, rank):
        first = jnp.sum(jnp.where(expert_ids == expert[None, :], base[:, None], 0), axis=0)
        return first + rank

    rt = route_t.astype(jnp.int32)
    pos = jnp.concatenate([rows_of(rt[0], rt[4]), rows_of(rt[1], rt[5])]).reshape(1, -1)
    n_active = (ends[-1] // tm).reshape(1)
    tile_start = jnp.arange(n_rows // tm, dtype=jnp.int32) * tm
    tile_start = jnp.minimum(tile_start, ends[-1] - tm)
    tile_expert = jnp.sum(tile_start[:, None] >= ends[None, :], axis=1).astype(jnp.int32)
    return pos, tile_expert, n_active


def kernel(x, c, w_ada, b_ada, norm_mix, norm_ffn, w_in, w_gk2, b_gk, q_gain, k_gain, gla_gain, w_out,
           w_router_grp, b_router_grp, w_router_exp, b_router_exp, w_gate, w_up, w_down):
    B, S, D = x.shape
    L = w_ada.shape[0]
    N = B * S

    mod_all = _ada_call(c, w_ada, b_ada).reshape(L, B, N_MOD, D)

    w_in_p = jnp.pad(w_in, ((0, 0), (0, 0), (0, D_IN_PAD - w_in.shape[-1]))).astype(BF16)
    w_gk2_p = jnp.pad(w_gk2, ((0, 0), (0, LANES - GLA_RANK), (0, 0))).astype(BF16)
    qg = jnp.tile(q_gain, (1, SB_HEADS)).reshape(L, 1, SB_WIDTH)
    kg = jnp.tile(k_gain, (1, SB_HEADS)).reshape(L, 1, SB_WIDTH)
    w_out_b = w_out.astype(BF16)
    w_r = jnp.swapaxes(jnp.concatenate([w_router_exp, w_router_grp], axis=-1), 1, 2)
    w_r = jnp.pad(w_r, ((0, 0), (0, ROUTER_ROWS - w_r.shape[1]), (0, 0)))
    w_r_hi = w_r.astype(BF16)
    w_r_lo = (w_r - w_r_hi.astype(F32)).astype(BF16)
    b_r = jnp.concatenate([b_router_exp, b_router_grp], axis=-1)
    b_r = jnp.pad(b_r, ((0, 0), (0, ROUTER_ROWS - b_r.shape[-1]))).reshape(L, ROUTER_ROWS, 1)
    wg = w_gate.reshape(L * N_EXPERTS, D, D_EXPERT)
    wu = w_up.reshape(L * N_EXPERTS, D, D_EXPERT)
    wd = w_down.reshape(L * N_EXPERTS, D_EXPERT, D)

    Bh = B // N_STREAMS
    Nh = Bh * S
    n_rows = 2 * Nh + N_EXPERTS * TM_MOE
    x_parts = x.reshape(N_STREAMS, Nh, D)

    def mixer(st, l):
        mod = mod_all[l, st["h"] * Bh:(st["h"] + 1) * Bh]
        outs = _proj_call(
            st["x"], st["moe"], mod, norm_mix[l].reshape(1, D), w_in_p, qg[l], kg[l], w_gk2_p[l],
            b_gk[l].reshape(1, GLA_K_WIDTH), S, l)
        xf = st["x"]
        if st["moe"] is not None:
            xf, outs = outs[0], outs[1:]
        q, k, v, gq, gk, gv, gr, lg = outs
        out_a = _sb_call(q, k, v, Bh, S)
        out_b = _gla_call(gq, gk, gv, gr, lg, gla_gain[l].reshape(1, GLA_DV), Bh, S)
        x1, h2a, h2b, route, route_t, counts = _out_call(out_a, out_b, xf, mod, norm_ffn[l].reshape(1, D),
                                                         w_out_b, w_r_hi, w_r_lo, b_r, S, l)
        pos, tile_expert, n_active = _routing_tables(route_t, counts, n_rows)
        st.update(x=x1, mod=mod, route=route, pos=pos, tile_expert=tile_expert + l * N_EXPERTS,
                  n_active=n_active, xs=_sc_dispatch((h2a, h2b), pos, n_rows))

    def experts(st):
        ys = _experts_call(st["tile_expert"], st["n_active"], st["xs"], wg, wu, wd)
        st["moe"] = (_sc_collect(ys, st["pos"]), st["route"], st["mod"])

    streams = [dict(h=h, x=x_parts[h], moe=None) for h in range(N_STREAMS)]
    for l in range(L):
        for st in streams:
            mixer(st, l)
        for st in streams:
            experts(st)
    outs = [_combine_call(st["x"], st["moe"][0], st["moe"][1], st["moe"][2], S) for st in streams]
    return jnp.concatenate(outs, axis=0).reshape(B, S, D)
```

```python
import functools

import jax
import jax.numpy as jnp
from jax import lax
from jax.experimental import pallas as pl
from jax.experimental.pallas import tpu as pltpu
from jax.experimental.pallas import tpu_sc as plsc

F32 = jnp.float32
BF16 = jnp.bfloat16

D_MODEL = 1024
SB_HEADS = 8
SB_HEAD_DIM = 64
SB_WIDTH = 512
GLA_HEADS = 4
GLA_DK = 64
GLA_DV = 128
GLA_K_WIDTH = 256
GLA_V_WIDTH = 512
GLA_RANK = 16
GLA_GATE_NORM = 16.0
GLA_CHUNK = 64
N_GROUPS = 4
EXPERTS_PER_GROUP = 8
N_EXPERTS = N_GROUPS * EXPERTS_PER_GROUP
D_EXPERT = 256
N_MOD = 6
ROUTER_ROWS = 48
EPS = 1e-6

LANES = 128
SUBLANES = 8
VMEM_LIMIT = 56 * 1024 * 1024

_C_Q, _C_K, _C_V = 0, 512, 1024
_C_GQ, _C_GK, _C_GV, _C_GR, _C_LR = 1536, 1792, 2048, 2560, 3072
D_IN_PAD = 3200

TM_PROJ = 1024
TM_OUT = 1024
TM_MOE = 512
SC_WINDOW = 128
D_PACK = D_MODEL // 4
U32 = jnp.uint32
SB_BQ = 256
SB_PAIRS = 4
SB_GROUP = 2
GLA_TS = 2048
GLA_SUB = 256
LOG2E = 1.4426950408889634
SB_SKIP_THRESH = 160.0


def _params(sem):
    return pltpu.CompilerParams(dimension_semantics=sem, vmem_limit_bytes=VMEM_LIMIT)


def _split_hi_lo(x):
    hi = x.astype(BF16)
    lo = (x - hi.astype(F32)).astype(BF16)
    return hi, lo


def _pack_rows(x):
    r = pltpu.bitcast(x.astype(BF16).astype(F32), U32)
    out = []
    for half in range(2):
        c0 = half * 2 * D_PACK
        out.append(r[:, c0:c0 + D_PACK] | (r[:, c0 + D_PACK:c0 + 2 * D_PACK] >> 16))
    return out


def _unpack_rows(pa, pb):
    parts = []
    for p in (pa, pb):
        parts.append(pltpu.bitcast(p & U32(0xFFFF0000), F32))
        parts.append(pltpu.bitcast(p << 16, F32))
    return jnp.concatenate(parts, axis=1)


def _softplus2(z2):
    return jnp.where(z2 > 64.0, z2, jnp.log2(1.0 + jnp.exp2(z2)))


def _ada_kernel(c_ref, w_ref, b_ref, o_ref):
    c = c_ref[...]
    ca = (c * (1.0 / (1.0 + jnp.exp(-c)))).astype(BF16)
    o_ref[...] = jnp.dot(ca, w_ref[...].astype(BF16), preferred_element_type=F32) + b_ref[...]


def _ada_call(c, w_ada, b_ada):
    L, D, _ = w_ada.shape
    B = c.shape[0]
    return pl.pallas_call(
        _ada_kernel,
        grid=(L, N_MOD),
        in_specs=[
            pl.BlockSpec((B, D), lambda l, j: (0, 0)),
            pl.BlockSpec((None, D, D), lambda l, j: (l, 0, j)),
            pl.BlockSpec((None, 1, D), lambda l, j: (l, 0, j)),
        ],
        out_specs=pl.BlockSpec((None, B, D), lambda l, j: (l, 0, j)),
        out_shape=jax.ShapeDtypeStruct((L, B, N_MOD * D), F32),
        compiler_params=_params(("parallel", "parallel")),
        name="ada_mod",
    )(c, w_ada, b_ada.reshape(L, 1, N_MOD * D))


def _moe_residual(x1_ref, y1a_ref, y1b_ref, y2a_ref, y2b_ref, route_ref, mod_ref):
    route = route_ref[...]
    lane = lax.broadcasted_iota(jnp.int32, route.shape, 1)
    w1 = jnp.sum(jnp.where(lane == 2, route, 0.0), axis=-1, keepdims=True)
    w2 = jnp.sum(jnp.where(lane == 3, route, 0.0), axis=-1, keepdims=True)
    y1 = _unpack_rows(y1a_ref[...], y1b_ref[...])
    y2 = _unpack_rows(y2a_ref[...], y2b_ref[...])
    return x1_ref[...] + mod_ref[5:6, :] * (w1 * y1 + w2 * y2)


def _proj_kernel(after_moe, *refs):
    if after_moe:
        moe_refs, refs = refs[:7], refs[7:]
    else:
        x_ref, refs = refs[0], refs[1:]
    mod_ref, g_ref, w_ref, qg_ref, kg_ref, wgk_ref, bgk_ref = refs[:7]
    outs = refs[7:]
    if after_moe:
        x = _moe_residual(*moe_refs)
        outs[0][...] = x
        outs = outs[1:]
    else:
        x = x_ref[...]
    q_ref, k_ref, v_ref, gq_ref, gk_ref, gv_ref, gr_ref, lg_ref = outs
    shift = mod_ref[0:1, :]
    scale = mod_ref[1:2, :]
    y = x * lax.rsqrt(jnp.mean(x * x, axis=-1, keepdims=True) + EPS) * g_ref[...]
    h = (y * (1.0 + scale) + shift).astype(BF16)

    def proj(c0, width):
        return jnp.dot(h, w_ref[:, c0:c0 + width], preferred_element_type=F32)

    r = lax.broadcasted_iota(jnp.int32, (SB_WIDTH, SB_WIDTH), 0) // SB_HEAD_DIM
    c = lax.broadcasted_iota(jnp.int32, (SB_WIDTH, SB_WIDTH), 1) // SB_HEAD_DIM
    avg = jnp.where(r == c, 1.0 / SB_HEAD_DIM, 0.0).astype(BF16)

    def head_norm(t, gain):
        ms = jnp.dot((t * t).astype(BF16), avg, preferred_element_type=F32)
        return t * lax.rsqrt(ms + EPS) * gain

    q = head_norm(proj(_C_Q, SB_WIDTH), qg_ref[...])
    q_ref[...] = (q * (SB_HEAD_DIM ** -0.5 * LOG2E)).astype(BF16)
    k_ref[...] = head_norm(proj(_C_K, SB_WIDTH), kg_ref[...]).astype(BF16)
    v_ref[...] = proj(_C_V, SB_WIDTH).astype(BF16)
    gq_ref[...] = proj(_C_GQ, GLA_K_WIDTH).astype(BF16)
    gk_ref[...] = proj(_C_GK, GLA_K_WIDTH).astype(BF16)
    gv_ref[...] = proj(_C_GV, GLA_V_WIDTH).astype(BF16)
    gr_ref[...] = proj(_C_GR, GLA_V_WIDTH).astype(BF16)
    lr = proj(_C_LR, LANES).astype(BF16)
    pre = jnp.dot(lr, wgk_ref[...], preferred_element_type=F32) + bgk_ref[...]
    lg_ref[...] = (jnp.minimum(pre, 0.0) - jnp.log(1.0 + jnp.exp(-jnp.abs(pre)))) * (1.0 / GLA_GATE_NORM)


def _proj_call(x, moe, mod, g, w_in, q_gain, k_gain, w_gk2, b_gk, seq, layer):
    N, D = x.shape
    tm = TM_PROJ
    per_b = seq // tm
    nblk = N // tm
    row = lambda i: (i, 0)
    const = lambda i: (0, 0)
    mod_spec = pl.BlockSpec((None, N_MOD, D), lambda i: (i // per_b, 0, 0))
    widths = [(SB_WIDTH, BF16)] * 3 + [(GLA_K_WIDTH, BF16)] * 2 + [(GLA_V_WIDTH, BF16)] * 2 + [(GLA_K_WIDTH, F32)]
    x_args, x_specs = [x], [pl.BlockSpec((tm, D), row)]
    if moe is not None:
        ys, route, mod_prev = moe
        dh = ys[0].shape[1]
        second = lambda i: (i + nblk, 0)
        x_args += [ys[0], ys[1], ys[0], ys[1], route, mod_prev]
        x_specs += [pl.BlockSpec((tm, dh), row), pl.BlockSpec((tm, dh), row), pl.BlockSpec((tm, dh), second),
                    pl.BlockSpec((tm, dh), second), pl.BlockSpec((tm, LANES), row), mod_spec]
        widths = [(D, F32)] + widths
    return pl.pallas_call(
        functools.partial(_proj_kernel, moe is not None),
        grid=(nblk,),
        in_specs=x_specs + [
            mod_spec,
            pl.BlockSpec((1, D), const),
            pl.BlockSpec((None, D, D_IN_PAD), lambda i: (layer, 0, 0)),
            pl.BlockSpec((1, SB_WIDTH), const),
            pl.BlockSpec((1, SB_WIDTH), const),
            pl.BlockSpec((LANES, GLA_K_WIDTH), const),
            pl.BlockSpec((1, GLA_K_WIDTH), const),
        ],
        out_specs=[pl.BlockSpec((tm, w), row) for w, _ in widths],
        out_shape=[jax.ShapeDtypeStruct((N, w), dt) for w, dt in widths],
        compiler_params=_params(("parallel",)),
        name="in_proj",
    )(*x_args, mod, g, w_in, q_gain, k_gain, w_gk2, b_gk)


def _sb_kernel(q_ref, k_ref, v_ref, o_ref):
    i = pl.program_id(2)
    bq = SB_BQ
    lane = lax.broadcasted_iota(jnp.int32, (1, LANES), 1)
    qms = []
    for pr in range(SB_PAIRS):
        q = q_ref[:, pr * LANES:(pr + 1) * LANES]
        for hh in range(2):
            in_head = (lane >= hh * SB_HEAD_DIM) & (lane < (hh + 1) * SB_HEAD_DIM)
            qms.append(jnp.where(in_head, q, jnp.zeros_like(q)))
    nh = len(qms)

    t_loc = lax.broadcasted_iota(jnp.int32, (bq, bq), 0)
    s_loc = lax.broadcasted_iota(jnp.int32, (bq, bq), 1)
    diag_mask = s_loc < t_loc
    later = jnp.where(t_loc > s_loc, 1.0, 0.0).astype(BF16)

    def steps(starts_and_masks, cs, accs):
        cs, accs = list(cs), list(accs)
        for h0 in range(0, nh, SB_GROUP):
            stage_group(starts_and_masks, range(h0, h0 + SB_GROUP), cs, accs)
        return cs, accs

    def stage_group(starts_and_masks, heads, cs, accs):
        pairs = [(s, hh) for s in range(len(starts_and_masks)) for hh in heads]
        block = lambda ref, s, hh: ref[pl.ds(starts_and_masks[s][0], bq), (hh // 2) * LANES:(hh // 2 + 1) * LANES]
        z = {(s, hh): lax.dot_general(qms[hh], block(k_ref, s, hh), (((1,), (1,)), ((), ())),
                                      preferred_element_type=F32) for s, hh in pairs}
        lf, logsig = {}, {}
        for s, hh in pairs:
            mask = starts_and_masks[s][1]
            sp = _softplus2(z[s, hh])
            logsig[s, hh] = z[s, hh] - sp
            lf[s, hh] = (sp if mask is None else jnp.where(mask, sp, 0.0)).astype(BF16)
        after = {p: jnp.dot(lf[p], later, preferred_element_type=F32) for p in pairs}
        w = {}
        for s, hh in pairs:
            mask = starts_and_masks[s][1]
            wt = jnp.exp2(logsig[s, hh] - after[s, hh] - cs[hh])
            w[s, hh] = (wt if mask is None else jnp.where(mask, wt, 0.0)).astype(BF16)
            cs[hh] = cs[hh] + after[s, hh][:, 0:1] + lf[s, hh][:, 0:1].astype(F32)
        for s, hh in pairs:
            accs[hh] = accs[hh] + jnp.dot(w[s, hh], block(v_ref, s, hh), preferred_element_type=F32)

    c_zero = [jnp.zeros((bq, 1), F32)] * nh
    acc_zero = [jnp.zeros((bq, LANES), F32)] * nh
    diag_start = pl.multiple_of(i * bq, bq)

    def diag_only():
        cs, accs = steps([(diag_start, diag_mask)], c_zero, acc_zero)
        return tuple(cs) + tuple(accs)

    def diag_and_previous():
        cs, accs = steps([(diag_start, diag_mask), (pl.multiple_of((i - 1) * bq, bq), None)], c_zero, acc_zero)
        return tuple(cs) + tuple(accs)

    state0 = lax.cond(i == 0, diag_only, diag_and_previous)

    def cond(state):
        c_min = functools.reduce(jnp.minimum, state[1:1 + nh])
        return jnp.logical_and(state[0] >= 0, jnp.min(c_min) < SB_SKIP_THRESH)

    def body(state):
        j = state[0]
        cs, accs = steps([(pl.multiple_of(j * bq, bq), None)], state[1:1 + nh], state[1 + nh:])
        return (j - 1,) + tuple(cs) + tuple(accs)

    final = lax.while_loop(cond, body, (i - 2,) + tuple(state0))
    accs = final[1 + nh:]
    for pr in range(SB_PAIRS):
        o_ref[:, pr * LANES:(pr + 1) * LANES] = jnp.where(
            lane < SB_HEAD_DIM, accs[2 * pr], accs[2 * pr + 1]).astype(BF16)


def _sb_call(q, k, v, batch, seq):
    N = q.shape[0]
    nq = seq // SB_BQ
    width = SB_PAIRS * LANES
    return pl.pallas_call(
        _sb_kernel,
        grid=(batch, SB_WIDTH // width, nq),
        in_specs=[
            pl.BlockSpec((SB_BQ, width), lambda b, p, i: (b * nq + i, p)),
            pl.BlockSpec((seq, width), lambda b, p, i: (b, p)),
            pl.BlockSpec((seq, width), lambda b, p, i: (b, p)),
        ],
        out_specs=pl.BlockSpec((SB_BQ, width), lambda b, p, i: (b * nq + i, p)),
        out_shape=jax.ShapeDtypeStruct((N, SB_WIDTH), BF16),
        compiler_params=_params(("parallel", "parallel", "arbitrary")),
        name="sb_attn",
    )(q, k, v)


def _gla_kernel(gq_ref, gk_ref, gv_ref, gr_ref, lg_ref, gain_ref, o_ref, st_ref):
    C, SUB, H = GLA_CHUNK, GLA_SUB, GLA_HEADS
    nt = (((1,), (1,)), ((), ()))

    @pl.when(pl.program_id(1) == 0)
    def _():
        st_ref[...] = jnp.zeros_like(st_ref)

    rows = lax.broadcasted_iota(jnp.int32, (SUB, SUB), 0)
    cols = lax.broadcasted_iota(jnp.int32, (SUB, SUB), 1)
    same_chunk = (rows // C) == (cols // C)
    causal = same_chunk & (rows >= cols)
    low = jnp.where(causal, 1.0, 0.0).astype(BF16)
    lane = lax.broadcasted_iota(jnp.int32, (1, GLA_K_WIDTH), 1)
    in_head = [(lane >= h * GLA_DK) & (lane < (h + 1) * GLA_DK) for h in range(H)]
    gain = gain_ref[...]

    st = st_ref[...]
    for sb in range(GLA_TS // SUB):
        r0 = sb * SUB
        hi, lo = _split_hi_lo(lg_ref[r0:r0 + SUB, :])
        b = jnp.dot(low, hi, preferred_element_type=F32) + jnp.dot(low, lo, preferred_element_type=F32)
        b_last = jnp.concatenate(
            [jnp.broadcast_to(b[c0 + C - 1:c0 + C, :], (C, GLA_K_WIDTH)) for c0 in range(0, SUB, C)], axis=0)
        gq = gq_ref[r0:r0 + SUB, :].astype(F32)
        gk = gk_ref[r0:r0 + SUB, :].astype(F32)
        q_dec = gq * jnp.exp(b) * (GLA_DK ** -0.5)
        k_inv = (gk * jnp.exp(-b)).astype(BF16)
        k_dec = (gk * jnp.exp(b_last - b)).astype(BF16)
        decay = jnp.exp(b_last)
        qh = [jnp.where(in_head[h], q_dec, 0.0).astype(BF16) for h in range(H)]
        kh = [jnp.where(in_head[h], k_dec, jnp.zeros_like(k_dec)) for h in range(H)]

        intra = []
        for h in range(H):
            s = lax.dot_general(qh[h], k_inv, nt, preferred_element_type=F32)
            s = jnp.where(causal, s, 0.0).astype(BF16)
            vh = gv_ref[r0:r0 + SUB, h * GLA_DV:(h + 1) * GLA_DV]
            intra.append(jnp.dot(s, vh, preferred_element_type=F32))

        kvs = []
        for c0 in range(0, SUB, C):
            kv = None
            for h in range(H):
                v_t = gv_ref[r0 + c0:r0 + c0 + C, h * GLA_DV:(h + 1) * GLA_DV].astype(F32).T.astype(BF16)
                part = jnp.dot(v_t, kh[h][c0:c0 + C], preferred_element_type=F32)
                kv = part if kv is None else kv + part
            kvs.append(kv)
        states = []
        for c, c0 in enumerate(range(0, SUB, C)):
            states.append(st.astype(BF16))
            st = st * decay[c0:c0 + 1, :] + kvs[c]
        inter = [[] for _ in range(H)]
        for c, c0 in enumerate(range(0, SUB, C)):
            for h in range(H):
                inter[h].append(lax.dot_general(qh[h][c0:c0 + C], states[c], nt, preferred_element_type=F32))

        for h in range(H):
            o = intra[h] + jnp.concatenate(inter[h], axis=0)
            y = o * lax.rsqrt(jnp.mean(o * o, axis=-1, keepdims=True) + EPS) * gain
            g = gr_ref[r0:r0 + SUB, h * GLA_DV:(h + 1) * GLA_DV].astype(F32)
            y = y * (g * (1.0 / (1.0 + jnp.exp(-g))))
            o_ref[r0:r0 + SUB, h * GLA_DV:(h + 1) * GLA_DV] = y.astype(BF16)
    st_ref[...] = st


def _gla_call(gq, gk, gv, gr, lg, gain, batch, seq):
    N = gq.shape[0]
    nt = seq // GLA_TS
    row = lambda b, t: (b * nt + t, 0)
    return pl.pallas_call(
        _gla_kernel,
        grid=(batch, nt),
        in_specs=[
            pl.BlockSpec((GLA_TS, GLA_K_WIDTH), row),
            pl.BlockSpec((GLA_TS, GLA_K_WIDTH), row),
            pl.BlockSpec((GLA_TS, GLA_V_WIDTH), row),
            pl.BlockSpec((GLA_TS, GLA_V_WIDTH), row),
            pl.BlockSpec((GLA_TS, GLA_K_WIDTH), row),
            pl.BlockSpec((1, GLA_DV), lambda b, t: (0, 0)),
        ],
        out_specs=pl.BlockSpec((GLA_TS, GLA_V_WIDTH), row),
        out_shape=jax.ShapeDtypeStruct((N, GLA_V_WIDTH), BF16),
        scratch_shapes=[pltpu.VMEM((GLA_DV, GLA_K_WIDTH), F32)],
        compiler_params=_params(("parallel", "arbitrary")),
        name="gla",
    )(gq, gk, gv, gr, lg, gain)


def _out_kernel(a_ref, b_ref, x_ref, mod_ref, g_ref, wa_ref, wb_ref, wrh_ref, wrl_ref, br_ref,
                x1_ref, h2a_ref, h2b_ref, route_ref, route_t_ref, cnt_ref, run_ref):
    @pl.when(pl.program_id(0) == 0)
    def _():
        run_ref[...] = jnp.zeros_like(run_ref)

    gate_m = mod_ref[2:3, :]
    shift_f = mod_ref[3:4, :]
    scale_f = mod_ref[4:5, :]
    y = jnp.dot(a_ref[...], wa_ref[...], preferred_element_type=F32)
    y = y + jnp.dot(b_ref[...], wb_ref[...], preferred_element_type=F32)
    x1 = x_ref[...] + gate_m * y
    x1_ref[...] = x1
    n = x1 * lax.rsqrt(jnp.mean(x1 * x1, axis=-1, keepdims=True) + EPS) * g_ref[...]
    h2 = n * (1.0 + scale_f) + shift_f
    h_hi, h_lo = _split_hi_lo(h2)
    h2a_ref[...], h2b_ref[...] = _pack_rows(h2)

    nt = (((1,), (1,)), ((), ()))
    wr_hi = wrh_ref[...]
    lt = lax.dot_general(wr_hi, h_hi, nt, preferred_element_type=F32)
    lt = lt + lax.dot_general(wr_hi, h_lo, nt, preferred_element_type=F32)
    lt = lt + lax.dot_general(wrl_ref[...], h_hi, nt, preferred_element_type=F32)
    lt = lt + br_ref[...]
    tm = lt.shape[1]
    row_of = lambda r: lt[r:r + 1, :]

    def first_argmax(vals):
        best = functools.reduce(jnp.maximum, vals)
        idx = jnp.full((1, tm), len(vals) - 1, jnp.int32)
        for j in reversed(range(len(vals) - 1)):
            idx = jnp.where(vals[j] == best, j, idx)
        return best, idx

    glog = [row_of(N_EXPERTS + j) for j in range(N_GROUPS)]
    gmax, gidx = first_argmax(glog)
    grp_w = 1.0 / functools.reduce(jnp.add, [jnp.exp(v - gmax) for v in glog])

    elog = []
    for e in range(EXPERTS_PER_GROUP):
        v = row_of((N_GROUPS - 1) * EXPERTS_PER_GROUP + e)
        for j in reversed(range(N_GROUPS - 1)):
            v = jnp.where(gidx == j, row_of(j * EXPERTS_PER_GROUP + e), v)
        elog.append(v)
    m1, i1 = first_argmax(elog)
    neg = jnp.float32(-jnp.inf)
    m2, i2 = first_argmax([jnp.where(i1 == e, neg, v) for e, v in enumerate(elog)])
    e21 = jnp.exp(m2 - m1)
    p1 = 1.0 / (1.0 + e21)
    p2 = e21 * p1
    ea = gidx * EXPERTS_PER_GROUP + i1
    eb = gidx * EXPERTS_PER_GROUP + i2

    rows = lax.broadcasted_iota(jnp.int32, (N_EXPERTS, tm), 0)
    sel_a = rows == ea
    sel_b = rows == eb
    onehot = jnp.where(sel_a | sel_b, 1.0, 0.0)
    rr = lax.broadcasted_iota(jnp.int32, (tm, tm), 0)
    cc = lax.broadcasted_iota(jnp.int32, (tm, tm), 1)
    earlier = jnp.where(rr < cc, 1.0, 0.0).astype(BF16)
    seen = jnp.dot(onehot.astype(BF16), earlier, preferred_element_type=F32) + run_ref[:, 0:1]
    rank_a = jnp.sum(jnp.where(sel_a, seen, 0.0), axis=0, keepdims=True)
    rank_b = jnp.sum(jnp.where(sel_b, seen, 0.0), axis=0, keepdims=True)
    run = run_ref[...] + jnp.sum(onehot, axis=1, keepdims=True)
    run_ref[...] = run
    cnt_ref[...] = run

    fields = (ea.astype(F32), eb.astype(F32), p1 * grp_w, p2 * grp_w, rank_a, rank_b)
    field_row = lax.broadcasted_iota(jnp.int32, (LANES, tm), 0)
    route_t = jnp.zeros((LANES, tm), F32)
    for idx, val in enumerate(fields):
        route_t = jnp.where(field_row == idx, val, route_t)
    route_t_ref[...] = route_t[:SUBLANES, :]
    route_ref[...] = route_t.T


def _out_call(oa, ob, x, mod, g, w_out, wr_hi, wr_lo, br, seq, layer):
    N, D = x.shape
    tm = TM_OUT
    per_b = seq // tm
    row = lambda i: (i, 0)
    const = lambda i: (0, 0)
    return pl.pallas_call(
        _out_kernel,
        grid=(N // tm,),
        in_specs=[
            pl.BlockSpec((tm, SB_WIDTH), row),
            pl.BlockSpec((tm, GLA_V_WIDTH), row),
            pl.BlockSpec((tm, D), row),
            pl.BlockSpec((None, N_MOD, D), lambda i: (i // per_b, 0, 0)),
            pl.BlockSpec((1, D), const),
            pl.BlockSpec((None, SB_WIDTH, D), lambda i: (layer, 0, 0)),
            pl.BlockSpec((None, GLA_V_WIDTH, D), lambda i: (layer, 1, 0)),
            pl.BlockSpec((None, ROUTER_ROWS, D), lambda i: (layer, 0, 0)),
            pl.BlockSpec((None, ROUTER_ROWS, D), lambda i: (layer, 0, 0)),
            pl.BlockSpec((None, ROUTER_ROWS, 1), lambda i: (layer, 0, 0)),
        ],
        out_specs=[pl.BlockSpec((tm, D), row), pl.BlockSpec((tm, D_PACK), row), pl.BlockSpec((tm, D_PACK), row),
                   pl.BlockSpec((tm, LANES), row), pl.BlockSpec((SUBLANES, tm), lambda i: (0, i)),
                   pl.BlockSpec((N_EXPERTS, LANES), const)],
        out_shape=[jax.ShapeDtypeStruct((N, D), F32), jax.ShapeDtypeStruct((N, D_PACK), U32),
                   jax.ShapeDtypeStruct((N, D_PACK), U32), jax.ShapeDtypeStruct((N, LANES), F32),
                   jax.ShapeDtypeStruct((SUBLANES, N), F32), jax.ShapeDtypeStruct((N_EXPERTS, LANES), F32)],
        scratch_shapes=[pltpu.VMEM((N_EXPERTS, LANES), F32)],
        compiler_params=_params(("arbitrary",)),
        name="out_proj_router",
    )(oa, ob, x, mod, g, w_out, w_out, wr_hi, wr_lo, br)


def _sc_mesh():
    return plsc.VectorSubcoreMesh(core_axis_name="c", subcore_axis_name="s")


def _sc_dispatch(halves, pos, n_rows):
    N, dh = halves[0].shape
    w = SC_WINDOW
    nblk = N // w
    out_type = [jax.ShapeDtypeStruct((n_rows, dh), h.dtype) for h in halves]

    @pl.kernel(out_type=out_type, mesh=_sc_mesh(), scratch_types=[], name="moe_dispatch")
    def run(xa_hbm, xb_hbm, i_hbm, oa_hbm, ob_hbm):
        for x_hbm, o_hbm in ((xa_hbm, oa_hbm), (xb_hbm, ob_hbm)):
            def body(x_vmem, i_vmem, o_hbm=o_hbm):
                pltpu.sync_copy(x_vmem, o_hbm.at[i_vmem.at[0]])

            pltpu.emit_pipeline(
                body,
                grid=(2 * nblk,),
                in_specs=[pl.BlockSpec((w, dh), lambda j: (j % nblk, 0)),
                          pl.BlockSpec((1, w), lambda j: (0, j))],
                out_specs=[],
                core_axis_name=("c", "s"),
                dimension_semantics=(pltpu.PARALLEL,),
            )(x_hbm, i_hbm)

    return run(halves[0], halves[1], pos)


def _sc_collect(halves, pos):
    dh = halves[0].shape[1]
    n_idx = pos.shape[1]
    w = SC_WINDOW
    out_type = [jax.ShapeDtypeStruct((n_idx, dh), h.dtype) for h in halves]

    @pl.kernel(out_type=out_type, mesh=_sc_mesh(), scratch_types=[], name="moe_collect")
    def run(ya_hbm, yb_hbm, i_hbm, oa_hbm, ob_hbm):
        for y_hbm, o_hbm in ((ya_hbm, oa_hbm), (yb_hbm, ob_hbm)):
            def body(i_vmem, o_vmem, y_hbm=y_hbm):
                pltpu.sync_copy(y_hbm.at[i_vmem.at[0]], o_vmem)

            pltpu.emit_pipeline(
                body,
                grid=(n_idx // w,),
                in_specs=[pl.BlockSpec((1, w), lambda j: (0, j))],
                out_specs=[pl.BlockSpec((w, dh), lambda j: (j, 0))],
                core_axis_name=("c", "s"),
                dimension_semantics=(pltpu.PARALLEL,),
            )(i_hbm, o_hbm)

    return run(halves[0], halves[1], pos)


def _experts_kernel(te_ref, na_ref, xa_ref, xb_ref, wg_ref, wu_ref, wd_ref, oa_ref, ob_ref):
    del te_ref

    @pl.when(pl.program_id(0) < na_ref[0])
    def _():
        x = _unpack_rows(xa_ref[...], xb_ref[...]).astype(BF16)
        a = jnp.dot(x, wg_ref[...].astype(BF16), preferred_element_type=F32)
        u = jnp.dot(x, wu_ref[...].astype(BF16), preferred_element_type=F32)
        act = (a * (1.0 / (1.0 + jnp.exp(-a)))) * u
        y = jnp.dot(act.astype(BF16), wd_ref[...].astype(BF16), preferred_element_type=F32)
        oa_ref[...], ob_ref[...] = _pack_rows(y)


def _experts_call(tile_expert, n_active, xs, wg, wu, wd):
    R, dh = xs[0].shape
    tm = TM_MOE
    rows = lambda i, te, na: (jnp.minimum(i, na[0] - 1), 0)
    grid_spec = pltpu.PrefetchScalarGridSpec(
        num_scalar_prefetch=2,
        grid=(R // tm,),
        in_specs=[
            pl.BlockSpec((tm, dh), rows),
            pl.BlockSpec((tm, dh), rows),
            pl.BlockSpec((None, D_MODEL, D_EXPERT), lambda i, te, na: (te[i], 0, 0)),
            pl.BlockSpec((None, D_MODEL, D_EXPERT), lambda i, te, na: (te[i], 0, 0)),
            pl.BlockSpec((None, D_EXPERT, D_MODEL), lambda i, te, na: (te[i], 0, 0)),
        ],
        out_specs=[pl.BlockSpec((tm, dh), rows), pl.BlockSpec((tm, dh), rows)],
    )
    return pl.pallas_call(
        _experts_kernel,
        grid_spec=grid_spec,
        out_shape=[jax.ShapeDtypeStruct((R, dh), U32)] * 2,
        compiler_params=_params(("arbitrary",)),
        name="moe_experts",
    )(tile_expert, n_active, xs[0], xs[1], wg, wu, wd)


def _combine_kernel(*refs):
    refs[-1][...] = _moe_residual(*refs[:-1])


def _combine_call(x1, ys, route, mod, seq):
    N, D = x1.shape
    tm = TM_OUT
    per_b = seq // tm
    nblk = N // tm
    dh = ys[0].shape[1]
    row = lambda i: (i, 0)
    second = lambda i: (i + nblk, 0)
    return pl.pallas_call(
        _combine_kernel,
        grid=(nblk,),
        in_specs=[
            pl.BlockSpec((tm, D), row),
            pl.BlockSpec((tm, dh), row),
            pl.BlockSpec((tm, dh), row),
            pl.BlockSpec((tm, dh), second),
            pl.BlockSpec((tm, dh), second),
            pl.BlockSpec((tm, LANES), row),
            pl.BlockSpec((None, N_MOD, D), lambda i: (i // per_b, 0, 0)),
        ],
        out_specs=pl.BlockSpec((tm, D), row),
        out_shape=jax.ShapeDtypeStruct((N, D), F32),
        compiler_params=_params(("parallel",)),
        name="moe_combine",
    )(x1, ys[0], ys[1], ys[0], ys[1], route, mod)


def _routing_tables(route_t, counts, n_rows):
    tm = TM_MOE
    cnt = counts[:, 0].astype(jnp.int32)
    padded = ((cnt + tm - 1) // tm) * tm
    ends = jnp.cumsum(padded)
    base = ends - padded
    expert_ids = jnp.arange(N_EXPERTS, dtype=jnp.int32)[:, None]

    def rows_of(expert, rank):
        first = jnp.sum(jnp.where(expert_ids == expert[None, :], base[:, None], 0), axis=0)
        return first + rank

    rt = route_t.astype(jnp.int32)
    pos = jnp.concatenate([rows_of(rt[0], rt[4]), rows_of(rt[1], rt[5])]).reshape(1, -1)
    n_active = (ends[-1] // tm).reshape(1)
    tile_start = jnp.arange(n_rows // tm, dtype=jnp.int32) * tm
    tile_start = jnp.minimum(tile_start, ends[-1] - tm)
    tile_expert = jnp.sum(tile_start[:, None] >= ends[None, :], axis=1).astype(jnp.int32)
    return pos, tile_expert, n_active


def kernel(x, c, w_ada, b_ada, norm_mix, norm_ffn, w_in, w_gk2, b_gk, q_gain, k_gain, gla_gain, w_out,
           w_router_grp, b_router_grp, w_router_exp, b_router_exp, w_gate, w_up, w_down):
    B, S, D = x.shape
    L = w_ada.shape[0]
    N = B * S

    mod_all = _ada_call(c, w_ada, b_ada).reshape(L, B, N_MOD, D)

    w_in_p = jnp.pad(w_in, ((0, 0), (0, 0), (0, D_IN_PAD - w_in.shape[-1]))).astype(BF16)
    w_gk2_p = jnp.pad(w_gk2, ((0, 0), (0, LANES - GLA_RANK), (0, 0))).astype(BF16)
    qg = jnp.tile(q_gain, (1, SB_HEADS)).reshape(L, 1, SB_WIDTH)
    kg = jnp.tile(k_gain, (1, SB_HEADS)).reshape(L, 1, SB_WIDTH)
    w_out_b = w_out.astype(BF16)
    w_r = jnp.swapaxes(jnp.concatenate([w_router_exp, w_router_grp], axis=-1), 1, 2)
    w_r = jnp.pad(w_r, ((0, 0), (0, ROUTER_ROWS - w_r.shape[1]), (0, 0)))
    w_r_hi = w_r.astype(BF16)
    w_r_lo = (w_r - w_r_hi.astype(F32)).astype(BF16)
    b_r = jnp.concatenate([b_router_exp, b_router_grp], axis=-1)
    b_r = jnp.pad(b_r, ((0, 0), (0, ROUTER_ROWS - b_r.shape[-1]))).reshape(L, ROUTER_ROWS, 1)
    wg = w_gate.reshape(L * N_EXPERTS, D, D_EXPERT)
    wu = w_up.reshape(L * N_EXPERTS, D, D_EXPERT)
    wd = w_down.reshape(L * N_EXPERTS, D_EXPERT, D)

    n_rows = 2 * N + N_EXPERTS * TM_MOE
    xf = x.reshape(N, D)
    moe = None
    for l in range(L):
        mod = mod_all[l]
        outs = _proj_call(
            xf, moe, mod, norm_mix[l].reshape(1, D), w_in_p, qg[l], kg[l], w_gk2_p[l],
            b_gk[l].reshape(1, GLA_K_WIDTH), S, l)
        if moe is not None:
            xf, outs = outs[0], outs[1:]
        q, k, v, gq, gk, gv, gr, lg = outs
        out_a = _sb_call(q, k, v, B, S)
        out_b = _gla_call(gq, gk, gv, gr, lg, gla_gain[l].reshape(1, GLA_DV), B, S)
        x1, h2a, h2b, route, route_t, counts = _out_call(out_a, out_b, xf, mod, norm_ffn[l].reshape(1, D),
                                          w_out_b, w_r_hi, w_r_lo, b_r, S, l)
        pos, tile_expert, n_active = _routing_tables(route_t, counts, n_rows)
        xs = _sc_dispatch((h2a, h2b), pos, n_rows)
        ys = _experts_call(tile_expert + l * N_EXPERTS, n_active, xs, wg, wu, wd)
        yt = _sc_collect(ys, pos)
        xf, moe = x1, (yt, route, mod)
    return _combine_call(xf, moe[0], moe[1], moe[2], S).reshape(B, S, D)
```
